```python
import math, functools
import jax, jax.numpy as jnp
from jax import lax
import numpy as np

D_MODEL = 2048
BATCH = 4
SEQ = 2048
DEPTH = 2
DEC_BATCH = 32
DEC_SEQ = 4
PAST_LEN = 8192
PAGE_SIZE = 128

N_EVEN = (DEPTH + 1) // 2
N_ODD = DEPTH // 2
D_CONV = D_MODEL // 2
CONV_WIDTH = 3
D_POOL = D_MODEL // 2
POOL_WINDOWS = (2, 4, 8, 16)
N_POOL_GROUPS = len(POOL_WINDOWS)
POOL_GROUP = D_POOL // N_POOL_GROUPS
POOL_BUF = max(POOL_WINDOWS) - 1
N_HEADS = 16
HEAD_DIM = D_MODEL // N_HEADS
D_ATTN = N_HEADS * HEAD_DIM
D_FF = -(-8 * D_MODEL // (3 * 256)) * 256
Q_BLOCK = 128
SB_BIAS_INIT = -8.0
ALPHA = (2.0 * DEPTH) ** 0.25
BETA = (8.0 * DEPTH) ** -0.25
LN_EPS = 1e-5

kernel_name = "conv_pool_stickbreak_deepnorm_decoder_step"


def layer_norm(x, g, b):
    xf = x.astype(jnp.float32)
    mu = jnp.mean(xf, axis=-1, keepdims=True)
    var = jnp.mean(jnp.square(xf - mu), axis=-1, keepdims=True)
    return ((xf - mu) * lax.rsqrt(var + LN_EPS) * g + b).astype(x.dtype)


def swiglu(x, w_gate, w_up, w_down):
    return (jax.nn.silu(x @ w_gate) * (x @ w_up)) @ w_down


def short_conv(u, buf, w_conv):
    L = u.shape[1]
    ext = jnp.concatenate([buf, u], axis=1)
    y = ext[:, 0:L] * w_conv[0]
    for kk in range(1, CONV_WIDTH):
        y = y + ext[:, kk:kk + L] * w_conv[kk]
    return y, ext[:, -(CONV_WIDTH - 1):]


def multiscale_pool(v, buf, pos0, w_pool, s_pool):
    B, L, _ = v.shape
    ext = jnp.concatenate([buf, v], axis=1)
    cs = jnp.cumsum(ext.astype(jnp.float32), axis=1)
    cs = jnp.pad(cs, ((0, 0), (1, 0), (0, 0)))
    cnt_pos = pos0 + jnp.arange(L) + 1
    vf = v.astype(jnp.float32)
    outs = []
    for g, w in enumerate(POOL_WINDOWS):
        sl = slice(g * POOL_GROUP, (g + 1) * POOL_GROUP)
        end = cs[:, POOL_BUF + 1:POOL_BUF + 1 + L, sl]
        start = cs[:, POOL_BUF + 1 - w:POOL_BUF + 1 - w + L, sl]
        cnt = jnp.minimum(w, cnt_pos).astype(jnp.float32)[None, :, None]
        outs.append((end - start) / cnt - vf[..., sl])
    d = jnp.concatenate(outs, axis=-1).reshape(B, L, N_POOL_GROUPS, POOL_GROUP).astype(v.dtype)
    y = jnp.einsum('blgc,gcd->blgd', d, w_pool).reshape(B, L, D_POOL) * s_pool
    return y, ext[:, -POOL_BUF:]


def conv_pool_mixer(x, conv_buf, pool_buf, pos0, w_in, w_conv, w_pool, s_pool, w_out):
    h = x @ w_in
    gb = h[..., :D_CONV]
    gc = h[..., D_CONV:2 * D_CONV]
    hv = h[..., 2 * D_CONV:3 * D_CONV]
    pv = h[..., 3 * D_CONV:]
    yc, new_conv = short_conv(gc * hv, conv_buf, w_conv)
    za = gb * yc
    zb, new_pool = multiscale_pool(pv, pool_buf, pos0, w_pool, s_pool)
    out = jnp.concatenate([za, zb], axis=-1) @ w_out
    return out, new_conv, new_pool


def sb_block(q, k, v, mask, c, bias):
    z = jnp.einsum('bqhd,bkhd->bhqk', q, k).astype(jnp.float32) * (HEAD_DIM ** -0.5)
    z = z + bias.astype(jnp.float32)[None, :, None, None]
    ls = jnp.where(mask, jax.nn.log_sigmoid(-z), 0.0)
    rc = lax.cumsum(ls, axis=3, reverse=True) + c[..., None]
    a = jnp.exp(jnp.where(mask, z + rc, -jnp.inf))
    o = jnp.einsum('bhqk,bkhd->bqhd', a, v.astype(jnp.float32))
    return o, c + jnp.sum(ls, axis=-1)


def sb_attention_prompt(q, k, v, bias):
    B, S = q.shape[0], q.shape[1]
    nb = S // Q_BLOCK
    qb = q.reshape(B, nb, Q_BLOCK, N_HEADS, HEAD_DIM).transpose(1, 0, 2, 3, 4)
    kpos = jnp.arange(S)

    def one_block(args):
        qi, i = args
        qpos = i * Q_BLOCK + jnp.arange(Q_BLOCK)
        mask = kpos[None, :] < qpos[:, None]
        o, _ = sb_block(qi, k, v, mask, jnp.zeros((B, N_HEADS, Q_BLOCK), jnp.float32), bias)
        return o

    o = lax.map(one_block, (qb, jnp.arange(nb)))
    return o.transpose(1, 0, 2, 3, 4).reshape(B, S, N_HEADS, HEAD_DIM)


def sb_attention_sample(q, k, v, bias, cache_k, cache_v, page_table, layer_idx):
    B, T = q.shape[0], q.shape[1]
    tpos = jnp.arange(T)
    mask_new = tpos[None, :] < tpos[:, None]
    o, c = sb_block(q, k, v, mask_new, jnp.zeros((B, N_HEADS, T), jnp.float32), bias)
    mask_page = jnp.ones((T, PAGE_SIZE), dtype=bool)

    def step(carry, pages):
        o_acc, c_acc = carry
        kp = cache_k[layer_idx, pages]
        vp = cache_v[layer_idx, pages]
        do, c_new = sb_block(q, kp, vp, mask_page, c_acc, bias)
        return (o_acc + do, c_new), None

    (o, c), _ = lax.scan(step, (o, c), page_table.T, reverse=True)
    return o


def sb_mixer(x, w_qkv, w_o, bias, attend):
    B, L, _ = x.shape
    h = x @ w_qkv
    q = h[..., :D_ATTN].reshape(B, L, N_HEADS, HEAD_DIM)
    k = h[..., D_ATTN:2 * D_ATTN].reshape(B, L, N_HEADS, HEAD_DIM)
    v = h[..., 2 * D_ATTN:].reshape(B, L, N_HEADS, HEAD_DIM)
    o = attend(q, k, v, bias).astype(x.dtype).reshape(B, L, D_ATTN)
    return o @ w_o, k, v


def setup_inputs(seed: int = 0) -> dict:
    key = jax.random.key(seed)
    ks = jax.random.split(key, 21)
    n_pages = PAST_LEN // PAGE_SIZE
    n_used = DEC_BATCH * n_pages
    n_pool_pages = (5 * n_used) // 4
    f32 = jnp.float32
    nrm = lambda k, shape, s: jax.random.normal(k, shape, f32) * s
    page_table = jax.random.permutation(ks[6], n_pool_pages)[:n_used].reshape(DEC_BATCH, n_pages).astype(jnp.int32)
    d_mix = D_CONV + D_POOL
    return {
        "x_prompt": nrm(ks[0], (BATCH, SEQ, D_MODEL), 1.0),
        "x_sample": nrm(ks[1], (DEC_BATCH, DEC_SEQ, D_MODEL), 1.0),
        "state_conv": nrm(ks[2], (N_EVEN, DEC_BATCH, CONV_WIDTH - 1, D_CONV), 1.0),
        "state_pool": nrm(ks[3], (N_EVEN, DEC_BATCH, POOL_BUF, D_POOL), 1.0),
        "cache_k": nrm(ks[4], (N_ODD, n_pool_pages, PAGE_SIZE, N_HEADS, HEAD_DIM), 1.0),
        "cache_v": nrm(ks[5], (N_ODD, n_pool_pages, PAGE_SIZE, N_HEADS, HEAD_DIM), 1.0),
        "page_table": page_table,
        "ln_g": 1.0 + nrm(ks[7], (DEPTH, 2, D_MODEL), 0.02),
        "ln_b": nrm(ks[8], (DEPTH, 2, D_MODEL), 0.02),
        "mix_w_in": nrm(ks[9], (N_EVEN, D_MODEL, 3 * D_CONV + D_POOL), D_MODEL ** -0.5),
        "conv_w": nrm(ks[10], (N_EVEN, CONV_WIDTH, D_CONV), CONV_WIDTH ** -0.5),
        "pool_w": nrm(ks[11], (N_EVEN, N_POOL_GROUPS, POOL_GROUP, POOL_GROUP), POOL_GROUP ** -0.5),
        "pool_scale": 1.0 + nrm(ks[12], (N_EVEN, D_POOL), 0.02),
        "mix_w_out": nrm(ks[13], (N_EVEN, d_mix, D_MODEL), BETA * d_mix ** -0.5),
        "attn_w_qkv": nrm(ks[14], (N_ODD, D_MODEL, 3 * D_ATTN), D_MODEL ** -0.5),
        "attn_w_o": nrm(ks[15], (N_ODD, D_ATTN, D_MODEL), BETA * D_ATTN ** -0.5),
        "attn_bias": SB_BIAS_INIT + nrm(ks[19], (N_ODD, N_HEADS), 0.1),
        "ffn_w_gate": nrm(ks[16], (DEPTH, D_MODEL, D_FF), D_MODEL ** -0.5),
        "ffn_w_up": nrm(ks[17], (DEPTH, D_MODEL, D_FF), D_MODEL ** -0.5),
        "ffn_w_down": nrm(ks[18], (DEPTH, D_FF, D_MODEL), BETA * D_FF ** -0.5),
    }


def reference(x_prompt, x_sample, state_conv, state_pool, cache_k, cache_v, page_table,
              ln_g, ln_b, mix_w_in, conv_w, pool_w, pool_scale, mix_w_out,
              attn_w_qkv, attn_w_o, attn_bias, ffn_w_gate, ffn_w_up, ffn_w_down):
    xp, xs = x_prompt, x_sample
    bp = xp.shape[0]
    conv_p, conv_s, pool_p, pool_s = [], [], [], []
    kp_l, vp_l, ks_l, vs_l = [], [], [], []
    for layer in range(DEPTH):
        if layer % 2 == 0:
            e = layer // 2
            zc = jnp.zeros((bp, CONV_WIDTH - 1, D_CONV), xp.dtype)
            zp = jnp.zeros((bp, POOL_BUF, D_POOL), xp.dtype)
            mp, cp_, pp_ = conv_pool_mixer(xp, zc, zp, 0, mix_w_in[e], conv_w[e], pool_w[e],
                                           pool_scale[e], mix_w_out[e])
            ms, cs_, ps_ = conv_pool_mixer(xs, state_conv[e].astype(xs.dtype), state_pool[e].astype(xs.dtype),
                                           PAST_LEN, mix_w_in[e], conv_w[e], pool_w[e],
                                           pool_scale[e], mix_w_out[e])
            conv_p.append(cp_); conv_s.append(cs_); pool_p.append(pp_); pool_s.append(ps_)
        else:
            o = layer // 2
            mp, kp, vp = sb_mixer(xp, attn_w_qkv[o], attn_w_o[o], attn_bias[o], sb_attention_prompt)
            attend_s = functools.partial(sb_attention_sample, cache_k=cache_k, cache_v=cache_v,
                                         page_table=page_table, layer_idx=o)
            ms, ks, vs = sb_mixer(xs, attn_w_qkv[o], attn_w_o[o], attn_bias[o], attend_s)
            kp_l.append(kp); vp_l.append(vp); ks_l.append(ks); vs_l.append(vs)
        xp = layer_norm(ALPHA * xp + mp, ln_g[layer, 0], ln_b[layer, 0])
        xs = layer_norm(ALPHA * xs + ms, ln_g[layer, 0], ln_b[layer, 0])
        xp = layer_norm(ALPHA * xp + swiglu(xp, ffn_w_gate[layer], ffn_w_up[layer], ffn_w_down[layer]),
                        ln_g[layer, 1], ln_b[layer, 1])
        xs = layer_norm(ALPHA * xs + swiglu(xs, ffn_w_gate[layer], ffn_w_up[layer], ffn_w_down[layer]),
                        ln_g[layer, 1], ln_b[layer, 1])
    return (xp, xs, jnp.stack(conv_p), jnp.stack(conv_s), jnp.stack(pool_p), jnp.stack(pool_s),
            jnp.stack(kp_l), jnp.stack(vp_l), jnp.stack(ks_l), jnp.stack(vs_l))
```

```python
import functools

import jax
import jax.numpy as jnp
from jax import lax
from jax.experimental import pallas as pl
from jax.experimental.pallas import tpu as pltpu

F32 = jnp.float32
BF16 = jnp.bfloat16

CONV_WIDTH = 3
POOL_WINDOWS = (2, 4, 8, 16)
POOL_BUF = max(POOL_WINDOWS) - 1
N_HEADS = 16
PAGE_SIZE = 128
DEPTH = 2
ALPHA = (2.0 * DEPTH) ** 0.25
LN_EPS = 1e-5

VMEM_CAP_BYTES = 56 * 1024 * 1024
SUBLANES = 8


def _params(n_axes, vmem_bytes):
    return pltpu.CompilerParams(
        dimension_semantics=("arbitrary",) * n_axes,
        vmem_limit_bytes=int(min(VMEM_CAP_BYTES, vmem_bytes)),
    )


def _nbytes(shape, dtype):
    n = 1
    for s in shape:
        n *= s
    return n * jnp.dtype(dtype).itemsize


def _mm_kernel(x_ref, w_ref, o_ref):
    o_ref[...] = jnp.dot(x_ref[...].astype(BF16), w_ref[...],
                         preferred_element_type=F32).astype(o_ref.dtype)


def matmul(x, w, out_dtype, tm, tn, name):
    m, k = x.shape
    n = w.shape[1]
    vmem = 2 * (_nbytes((tm, k), x.dtype) + _nbytes((k, tn), w.dtype) + _nbytes((tm, tn), out_dtype))
    vmem += _nbytes((tm, k), BF16) + 2 * _nbytes((tm, tn), F32)
    return pl.pallas_call(
        _mm_kernel,
        grid=(n // tn, m // tm),
        in_specs=[pl.BlockSpec((tm, k), lambda j, i: (i, 0)),
                  pl.BlockSpec((k, tn), lambda j, i: (0, j))],
        out_specs=pl.BlockSpec((tm, tn), lambda j, i: (i, j)),
        out_shape=jax.ShapeDtypeStruct((m, n), out_dtype),
        compiler_params=_params(2, vmem + (4 << 20)),
        name=name,
    )(x, w)


def _layer_norm_rows(s, g, b):
    mu = jnp.mean(s, axis=-1, keepdims=True)
    d = s - mu
    var = jnp.mean(d * d, axis=-1, keepdims=True)
    return d * lax.rsqrt(var + LN_EPS) * g + b


def _mm_ln_kernel(x_ref, w_ref, r_ref, g_ref, b_ref, y_ref, yb_ref):
    acc = jnp.dot(x_ref[...].astype(BF16), w_ref[...], preferred_element_type=F32)
    y = _layer_norm_rows(ALPHA * r_ref[...] + acc, g_ref[...], b_ref[...])
    y_ref[...] = y
    yb_ref[...] = y.astype(BF16)


def matmul_residual_ln(x, w, resid, g, b, tm, name):
    m, k = x.shape
    d = w.shape[1]
    vmem = 2 * (_nbytes((tm, k), x.dtype) + _nbytes((k, d), w.dtype) + 2 * _nbytes((tm, d), F32)
                + _nbytes((tm, d), BF16)) + 4 * _nbytes((tm, d), F32)
    row = lambda i: (i, 0)
    const = lambda i: (0, 0)
    return pl.pallas_call(
        _mm_ln_kernel,
        grid=(m // tm,),
        in_specs=[pl.BlockSpec((tm, k), row), pl.BlockSpec((k, d), const),
                  pl.BlockSpec((tm, d), row), pl.BlockSpec((1, d), const), pl.BlockSpec((1, d), const)],
        out_specs=[pl.BlockSpec((tm, d), row), pl.BlockSpec((tm, d), row)],
        out_shape=[jax.ShapeDtypeStruct((m, d), F32), jax.ShapeDtypeStruct((m, d), BF16)],
        compiler_params=_params(1, vmem + (4 << 20)),
        name=name,
    )(x, w, resid, g, b)


def _ffn_up_kernel(x_ref, wg_ref, wu_ref, h_ref):
    x = x_ref[...]
    gate = jnp.dot(x, wg_ref[...], preferred_element_type=F32)
    up = jnp.dot(x, wu_ref[...], preferred_element_type=F32)
    h_ref[...] = (jax.nn.silu(gate) * up).astype(h_ref.dtype)


def ffn_up(x, wg, wu, tm, tf, name):
    m, k = x.shape
    f = wg.shape[1]
    vmem = 2 * (_nbytes((tm, k), x.dtype) + 2 * _nbytes((k, tf), wg.dtype) + _nbytes((tm, tf), BF16))
    vmem += 4 * _nbytes((tm, tf), F32)
    return pl.pallas_call(
        _ffn_up_kernel,
        grid=(f // tf, m // tm),
        in_specs=[pl.BlockSpec((tm, k), lambda j, i: (i, 0)),
                  pl.BlockSpec((k, tf), lambda j, i: (0, j)),
                  pl.BlockSpec((k, tf), lambda j, i: (0, j))],
        out_specs=pl.BlockSpec((tm, tf), lambda j, i: (i, j)),
        out_shape=jax.ShapeDtypeStruct((m, f), BF16),
        compiler_params=_params(2, vmem + (4 << 20)),
        name=name,
    )(x, wg, wu)


def _ffn_down_ln_kernel(h_ref, w_ref, r_ref, g_ref, b_ref, y_ref, yb_ref, acc_ref):
    kk = pl.program_id(1)

    @pl.when(kk == 0)
    def _():
        acc_ref[...] = jnp.zeros_like(acc_ref)

    acc_ref[...] += jnp.dot(h_ref[...], w_ref[...], preferred_element_type=F32)

    @pl.when(kk == pl.num_programs(1) - 1)
    def _():
        y = _layer_norm_rows(ALPHA * r_ref[...] + acc_ref[...], g_ref[...], b_ref[...])
        y_ref[...] = y
        yb_ref[...] = y.astype(BF16)


def ffn_down_residual_ln(h, w, resid, g, b, tm, tk, name):
    m, f = h.shape
    d = w.shape[1]
    vmem = 2 * (_nbytes((tm, tk), h.dtype) + _nbytes((tk, d), w.dtype) + 2 * _nbytes((tm, d), F32)
                + _nbytes((tm, d), BF16)) + 5 * _nbytes((tm, d), F32)
    row = lambda i, kk: (i, 0)
    const = lambda i, kk: (0, 0)
    return pl.pallas_call(
        _ffn_down_ln_kernel,
        grid=(m // tm, f // tk),
        in_specs=[pl.BlockSpec((tm, tk), lambda i, kk: (i, kk)),
                  pl.BlockSpec((tk, d), lambda i, kk: (kk, 0)),
                  pl.BlockSpec((tm, d), row), pl.BlockSpec((1, d), const), pl.BlockSpec((1, d), const)],
        out_specs=[pl.BlockSpec((tm, d), row), pl.BlockSpec((tm, d), row)],
        out_shape=[jax.ShapeDtypeStruct((m, d), F32), jax.ShapeDtypeStruct((m, d), BF16)],
        scratch_shapes=[pltpu.VMEM((tm, d), F32)],
        compiler_params=_params(2, vmem + (4 << 20)),
        name=name,
    )(h, w, resid, g, b)


def _mixer_prompt_kernel(h_ref, wc_ref, wp_ref, sp_ref, z_ref, co_ref, po_ref, cbuf, pbuf, *, ts, dc):
    i = pl.program_id(1)
    ch, ph = SUBLANES, 2 * SUBLANES
    dp = pbuf.shape[1]
    pg = dp // len(POOL_WINDOWS)

    @pl.when(i == 0)
    def _():
        cbuf[0:ch, :] = jnp.zeros((ch, dc), F32)
        pbuf[0:ph, :] = jnp.zeros((ph, dp), F32)

    gb = h_ref[:, 0:dc]
    u = h_ref[:, dc:2 * dc] * h_ref[:, 2 * dc:3 * dc]
    pv = h_ref[:, 3 * dc:3 * dc + dp]
    cbuf[ch:ch + ts, :] = u
    pbuf[ph:ph + ts, :] = pv

    y = cbuf[ch - 2:ch - 2 + ts, :] * wc_ref[0:1, :]
    y = y + cbuf[ch - 1:ch - 1 + ts, :] * wc_ref[1:2, :]
    y = y + u * wc_ref[2:3, :]
    z_ref[:, 0:dc] = (gb * y).astype(z_ref.dtype)

    pos = i * ts + lax.broadcasted_iota(jnp.int32, (ts, 1), 0)
    for gi, w in enumerate(POOL_WINDOWS):
        lo = gi * pg
        cur = pv[:, lo:lo + pg]
        s = cur
        for back in range(1, w):
            s = s + pbuf[ph - back:ph - back + ts, lo:lo + pg]
        cnt = jnp.minimum(w, pos + 1).astype(F32)
        dlt = s / cnt - cur
        yg = jnp.dot(dlt.astype(BF16), wp_ref[gi], preferred_element_type=F32)
        z_ref[:, dc + lo:dc + lo + pg] = (yg * sp_ref[:, lo:lo + pg]).astype(z_ref.dtype)

    co_ref[...] = cbuf[ch + ts - (CONV_WIDTH - 1):ch + ts, :]
    po_ref[...] = pbuf[ph + ts - POOL_BUF:ph + ts, :]
    cbuf[0:ch, :] = cbuf[ts:ts + ch, :]
    pbuf[0:ph, :] = pbuf[ts:ts + ph, :]


def mixer_prompt(h, wc, wp, sp, batch, seq, ts):
    rows, width = h.shape
    dp = sp.shape[1]
    dc = (width - dp) // 3
    nt = seq // ts
    vmem = 2 * (_nbytes((ts, width), F32) + _nbytes((ts, dc + dp), BF16)) + 2 * _nbytes(wp.shape, BF16)
    vmem += _nbytes((ts + 8, dc), F32) + _nbytes((ts + 16, dp), F32) + 6 * _nbytes((ts, dc), F32)
    return pl.pallas_call(
        functools.partial(_mixer_prompt_kernel, ts=ts, dc=dc),
        grid=(batch, nt),
        in_specs=[pl.BlockSpec((ts, width), lambda b, i: (b * nt + i, 0)),
                  pl.BlockSpec(wc.shape, lambda b, i: (0, 0)),
                  pl.BlockSpec(wp.shape, lambda b, i: (0, 0, 0)),
                  pl.BlockSpec(sp.shape, lambda b, i: (0, 0))],
        out_specs=[pl.BlockSpec((ts, dc + dp), lambda b, i: (b * nt + i, 0)),
                   pl.BlockSpec((None, CONV_WIDTH - 1, dc), lambda b, i: (b, 0, 0)),
                   pl.BlockSpec((None, POOL_BUF, dp), lambda b, i: (b, 0, 0))],
        out_shape=[jax.ShapeDtypeStruct((rows, dc + dp), BF16),
                   jax.ShapeDtypeStruct((batch, CONV_WIDTH - 1, dc), F32),
                   jax.ShapeDtypeStruct((batch, POOL_BUF, dp), F32)],
        scratch_shapes=[pltpu.VMEM((ts + SUBLANES, dc), F32), pltpu.VMEM((ts + 2 * SUBLANES, dp), F32)],
        compiler_params=_params(2, vmem + (4 << 20)),
        name="mixer_prompt",
    )(h, wc, wp, sp)


def _mixer_sample_kernel(h_ref, sc_ref, spool_ref, wc_ref, wp_ref, sp_ref, z_ref, co_ref, po_ref,
                         *, nb, nt, dc, pos0):
    dp = sp_ref.shape[1]
    pg = dp // len(POOL_WINDOWS)
    nc = CONV_WIDTH - 1

    conv_ext = [sc_ref[j] for j in range(nc)]
    pool_ext = [spool_ref[j] for j in range(POOL_BUF)]
    gbs = []
    for t in range(nt):
        gbs.append(h_ref[t, :, 0:dc])
        conv_ext.append(h_ref[t, :, dc:2 * dc] * h_ref[t, :, 2 * dc:3 * dc])
        pool_ext.append(h_ref[t, :, 3 * dc:3 * dc + dp])

    for t in range(nt):
        y = conv_ext[t] * wc_ref[0:1, :]
        for kk in range(1, CONV_WIDTH):
            y = y + conv_ext[t + kk] * wc_ref[kk:kk + 1, :]
        z_ref[t, :, 0:dc] = gbs[t] * y
    for j in range(nc):
        co_ref[j] = conv_ext[nt + j]

    for gi, w in enumerate(POOL_WINDOWS):
        lo = gi * pg
        dls = []
        for t in range(nt):
            e = POOL_BUF + t
            s = pool_ext[e][:, lo:lo + pg]
            for back in range(1, w):
                s = s + pool_ext[e - back][:, lo:lo + pg]
            cnt = float(min(w, pos0 + t + 1))
            dls.append(s / cnt - pool_ext[e][:, lo:lo + pg])
        dl = jnp.concatenate(dls, axis=0).astype(BF16)
        yg = jnp.dot(dl, wp_ref[gi], preferred_element_type=F32) * sp_ref[:, lo:lo + pg]
        for t in range(nt):
            z_ref[t, :, dc + lo:dc + lo + pg] = yg[t * nb:(t + 1) * nb, :]
    for j in range(POOL_BUF):
        po_ref[j] = pool_ext[nt + j]


def mixer_sample(h, state_conv, state_pool, wc, wp, sp, nb, nt, pos0):
    rows, width = h.shape
    dp = sp.shape[1]
    dc = (width - dp) // 3
    nc = CONV_WIDTH - 1
    step_major = lambda a: jnp.swapaxes(a, 0, 1)
    z, conv_new, pool_new = pl.pallas_call(
        functools.partial(_mixer_sample_kernel, nb=nb, nt=nt, dc=dc, pos0=pos0),
        out_shape=[jax.ShapeDtypeStruct((nt, nb, dc + dp), F32),
                   jax.ShapeDtypeStruct((nc, nb, dc), F32),
                   jax.ShapeDtypeStruct((POOL_BUF, nb, dp), F32)],
        compiler_params=pltpu.CompilerParams(vmem_limit_bytes=32 << 20),
        name="mixer_sample",
    )(step_major(h.reshape(nb, nt, width)), step_major(state_conv), step_major(state_pool), wc, wp, sp)
    return step_major(z).reshape(rows, dc + dp), step_major(conv_new), step_major(pool_new)


def _neg_softplus(z):
    return -(jnp.maximum(z, 0.0) + jnp.log1p(jnp.exp(-jnp.abs(z))))


def _suffix_sum(ls, tri):
    hi = ls.astype(BF16)
    lo = (ls - hi.astype(F32)).astype(BF16)
    return (jnp.dot(hi, tri, preferred_element_type=F32)
            + jnp.dot(lo, tri, preferred_element_type=F32))


def _lower_tri(n):
    r = lax.broadcasted_iota(jnp.int32, (n, n), 0)
    c = lax.broadcasted_iota(jnp.int32, (n, n), 1)
    return jnp.where(r >= c, 1.0, 0.0).astype(BF16)


def _sb_prompt_kernel(bias_ref, q_ref, k_ref, v_ref, o_ref, kb_ref, vb_ref, *, tb, scale):
    seq, dh = q_ref.shape
    nblk = seq // tb
    bias = bias_ref[pl.program_id(1)]
    kb_ref[...] = k_ref[...].astype(BF16)
    vb_ref[...] = v_ref[...].astype(BF16)
    tri = _lower_tri(tb)
    r = lax.broadcasted_iota(jnp.int32, (tb, tb), 0)
    c_ = lax.broadcasted_iota(jnp.int32, (tb, tb), 1)
    causal = c_ < r

    def pair(q, j, carry, masked):
        c, o = carry
        start = pl.multiple_of(j * tb, tb)
        kj = kb_ref[pl.ds(start, tb), :]
        vj = vb_ref[pl.ds(start, tb), :]
        z = lax.dot_general(q, kj, (((1,), (1,)), ((), ())), preferred_element_type=F32) * scale + bias
        ls = _neg_softplus(z)
        if masked:
            ls = jnp.where(causal, ls, 0.0)
        rc = _suffix_sum(ls, tri)
        a = jnp.exp(z + rc + c)
        if masked:
            a = jnp.where(causal, a, 0.0)
        o = o + jnp.dot(a.astype(BF16), vj, preferred_element_type=F32)
        return c + rc[:, 0:1], o

    for i in range(nblk):
        q = q_ref[i * tb:(i + 1) * tb, :].astype(BF16)
        carry = (jnp.zeros((tb, 1), F32), jnp.zeros((tb, dh), F32))
        carry = pair(q, i, carry, True)
        if i > 0:
            carry = lax.fori_loop(0, i, lambda jj, cr: pair(q, i - 1 - jj, cr, False), carry)
        o_ref[i * tb:(i + 1) * tb, :] = carry[1].astype(o_ref.dtype)


def sb_attention_prompt(q, k, v, bias, batch, seq, tb):
    rows, d_attn = q.shape
    dh = d_attn // N_HEADS
    blk = pl.BlockSpec((seq, dh), lambda b, h: (b, h))
    vmem = 2 * (3 * _nbytes((seq, dh), F32) + _nbytes((seq, dh), BF16)) + 2 * _nbytes((seq, dh), BF16)
    vmem += 16 * _nbytes((tb, tb), F32)
    return pl.pallas_call(
        functools.partial(_sb_prompt_kernel, tb=tb, scale=dh ** -0.5),
        grid=(batch, N_HEADS),
        in_specs=[pl.BlockSpec(memory_space=pltpu.SMEM), blk, blk, blk],
        out_specs=blk,
        out_shape=jax.ShapeDtypeStruct((rows, d_attn), BF16),
        scratch_shapes=[pltpu.VMEM((seq, dh), BF16), pltpu.VMEM((seq, dh), BF16)],
        compiler_params=_params(2, vmem + (4 << 20)),
        name="sb_attention_prompt",
    )(bias, q, k, v)


def _sb_sample_kernel(pt_ref, q_ref, kn_ref, vn_ref, bias_ref, *refs, n_pg, scale):
    del pt_ref
    k_refs = refs[:n_pg]
    v_refs = refs[n_pg:2 * n_pg]
    o_ref, qf_ref, c_ref, acc_ref = refs[2 * n_pg:]
    nt, nh, dh = kn_ref.shape
    rpad = qf_ref.shape[0] // nh
    g = pl.program_id(1)
    bias_col = bias_ref[...]

    @pl.when(g == 0)
    def _():
        for h in range(nh):
            qf_ref[h * rpad:(h + 1) * rpad, :] = q_ref[:, h * dh:(h + 1) * dh]
        rr = lax.broadcasted_iota(jnp.int32, (nh * rpad, nh), 0)
        hh = lax.broadcasted_iota(jnp.int32, (nh * rpad, nh), 1)
        rep = jnp.where(rr // rpad == hh, 1.0, 0.0).astype(BF16)
        step = lax.broadcasted_iota(jnp.int32, (nh * rpad, 1), 0) % rpad
        qr = qf_ref[...].astype(BF16).astype(F32)
        zs, lss, valids = [], [], []
        for s in range(nt):
            kr = jnp.dot(rep, kn_ref[s].astype(BF16), preferred_element_type=F32)
            z = jnp.sum(qr * kr, axis=1, keepdims=True) * scale + bias_col
            valid = step > s
            zs.append(z)
            valids.append(valid)
            lss.append(jnp.where(valid, _neg_softplus(z), 0.0))
        acc = jnp.zeros(acc_ref.shape, F32)
        rc = jnp.zeros((nh * rpad, 1), F32)
        for s in reversed(range(nt)):
            rc = rc + lss[s]
            a = jnp.where(valids[s], jnp.exp(zs[s] + rc), 0.0)
            vr = jnp.dot(rep, vn_ref[s].astype(BF16), preferred_element_type=F32)
            acc = acc + a.astype(BF16).astype(F32) * vr
        acc_ref[...] = acc
        c_ref[...] = rc

    width = n_pg * PAGE_SIZE
    zrows = []
    for h in range(nh):
        qh = qf_ref[h * rpad:(h + 1) * rpad, :].astype(BF16)
        kh = jnp.concatenate([k_refs[p][:, h, :] for p in range(n_pg)], axis=0).astype(BF16)
        zrows.append(lax.dot_general(qh, kh, (((1,), (1,)), ((), ())), preferred_element_type=F32))
    z = jnp.concatenate(zrows, axis=0) * scale + bias_col
    ls = _neg_softplus(z)
    tb = 2 * PAGE_SIZE
    tri = _lower_tri(tb)
    c = c_ref[...]
    a_blocks = [None] * (width // tb)
    for blk in reversed(range(width // tb)):
        rc = _suffix_sum(ls[:, blk * tb:(blk + 1) * tb], tri)
        a_blocks[blk] = jnp.exp(z[:, blk * tb:(blk + 1) * tb] + rc + c).astype(BF16)
        c = c + rc[:, 0:1]
    c_ref[...] = c
    a = jnp.concatenate(a_blocks, axis=1)
    orows = []
    for h in range(nh):
        vh = jnp.concatenate([v_refs[p][:, h, :] for p in range(n_pg)], axis=0).astype(BF16)
        orows.append(jnp.dot(a[h * rpad:(h + 1) * rpad, :], vh, preferred_element_type=F32))
    acc_ref[...] += jnp.concatenate(orows, axis=0)

    @pl.when(g == pl.num_programs(1) - 1)
    def _():
        for h in range(nh):
            o_ref[:, h * dh:(h + 1) * dh] = acc_ref[h * rpad:(h + 1) * rpad, :]


def sb_attention_sample(q, k_new, v_new, bias, cache_k, cache_v, page_table, n_pg):
    nb, nt, d_attn = q.shape
    nh, dh = k_new.shape[2], k_new.shape[3]
    n_pages = page_table.shape[1]
    rpad = SUBLANES
    q_pad = jnp.pad(q, ((0, 0), (0, rpad - nt), (0, 0)))
    bias_col = jnp.repeat(bias.astype(F32), rpad).reshape(nh * rpad, 1)
    n_steps = n_pages // n_pg

    def page_spec(p):
        return pl.BlockSpec(
            (None, PAGE_SIZE, nh, dh),
            lambda b, g, pt: (pt[b, n_pages - (g + 1) * n_pg + p], 0, 0, 0))

    per_seq3 = lambda b, g, pt: (b, 0, 0)
    per_seq4 = lambda b, g, pt: (b, 0, 0, 0)
    grid_spec = pltpu.PrefetchScalarGridSpec(
        num_scalar_prefetch=1,
        grid=(nb, n_steps),
        in_specs=[pl.BlockSpec((None, rpad, d_attn), per_seq3),
                  pl.BlockSpec((None, nt, nh, dh), per_seq4),
                  pl.BlockSpec((None, nt, nh, dh), per_seq4),
                  pl.BlockSpec((nh * rpad, 1), lambda b, g, pt: (0, 0))]
                 + [page_spec(p) for p in range(n_pg)] * 2,
        out_specs=pl.BlockSpec((None, rpad, d_attn), per_seq3),
        scratch_shapes=[pltpu.VMEM((nh * rpad, dh), F32), pltpu.VMEM((nh * rpad, 1), F32),
                        pltpu.VMEM((nh * rpad, dh), F32)],
    )
    vmem = 2 * 2 * n_pg * _nbytes((PAGE_SIZE, nh, dh), F32) + (12 << 20)
    return pl.pallas_call(
        functools.partial(_sb_sample_kernel, n_pg=n_pg, scale=dh ** -0.5),
        grid_spec=grid_spec,
        out_shape=jax.ShapeDtypeStruct((nb, rpad, d_attn), F32),
        compiler_params=_params(2, vmem),
        name="sb_attention_sample",
    )(page_table, q_pad, k_new, v_new, bias_col, *([cache_k] * n_pg), *([cache_v] * n_pg))


def _ffn_block(x, xb, wg, wu, wd, g, b, tm_up, tm_down, tag):
    h = ffn_up(xb, wg, wu, tm_up, 512, "ffn_up_" + tag)
    return ffn_down_residual_ln(h, wd, x, g, b, tm_down, 512, "ffn_down_ln_" + tag)


def kernel(x_prompt, x_sample, state_conv, state_pool, cache_k, cache_v, page_table, ln_g, ln_b, mix_w_in, conv_w, pool_w, pool_scale, mix_w_out, attn_w_qkv, attn_w_o, attn_bias, ffn_w_gate, ffn_w_up, ffn_w_down):
    bp, seq, d = x_prompt.shape
    bs, ts_, _ = x_sample.shape
    mp, ms = bp * seq, bs * ts_
    past_len = page_table.shape[1] * PAGE_SIZE
    xp = x_prompt.reshape(mp, d)
    xs = x_sample.reshape(ms, d)
    dp = pool_scale.shape[1]
    dc = conv_w.shape[2]
    d_attn = attn_w_o.shape[1]
    dh = d_attn // N_HEADS

    def ln_params(layer, idx):
        return ln_g[layer, idx].reshape(1, d), ln_b[layer, idx].reshape(1, d)

    w_in = mix_w_in[0].astype(BF16)
    w_out = mix_w_out[0].astype(BF16)
    wp = pool_w[0].astype(BF16)
    sp = pool_scale[0].reshape(1, dp)
    hp = matmul(xp, w_in, F32, 512, 1024, "mix_in_prompt")
    hs = matmul(xs, w_in, F32, ms, 1024, "mix_in_sample")
    zp, conv_p, pool_p = mixer_prompt(hp, conv_w[0], wp, sp, bp, seq, 256)
    zs, conv_s, pool_s = mixer_sample(hs, state_conv[0], state_pool[0], conv_w[0], wp, sp, bs, ts_, past_len)
    g0, b0 = ln_params(0, 0)
    xp, xpb = matmul_residual_ln(zp, w_out, xp, g0, b0, 256, "mix_out_ln_prompt")
    xs, xsb = matmul_residual_ln(zs, w_out, xs, g0, b0, ms, "mix_out_ln_sample")
    g1, b1 = ln_params(0, 1)
    wg, wu, wd = (w[0].astype(BF16) for w in (ffn_w_gate, ffn_w_up, ffn_w_down))
    xp, xpb = _ffn_block(xp, xpb, wg, wu, wd, g1, b1, 1024, 512, "l0_prompt")
    xs, xsb = _ffn_block(xs, xsb, wg, wu, wd, g1, b1, ms, ms, "l0_sample")

    w_q, w_k, w_v = (attn_w_qkv[0][:, i * d_attn:(i + 1) * d_attn].astype(BF16) for i in range(3))
    w_o = attn_w_o[0].astype(BF16)
    bias = attn_bias[0]
    qp, kp, vp = (matmul(xpb, w, F32, 1024, 1024, "qkv_prompt") for w in (w_q, w_k, w_v))
    qs, ks, vs = (matmul(xsb, w, F32, ms, 1024, "qkv_sample") for w in (w_q, w_k, w_v))
    op = sb_attention_prompt(qp, kp, vp, bias, bp, seq, 256)
    ks4 = ks.reshape(bs, ts_, N_HEADS, dh)
    vs4 = vs.reshape(bs, ts_, N_HEADS, dh)
    o_pad = sb_attention_sample(qs.reshape(bs, ts_, d_attn), ks4, vs4, bias, cache_k[0], cache_v[0],
                                page_table, 8)
    os_ = o_pad[:, :ts_, :].reshape(ms, d_attn)
    g0, b0 = ln_params(1, 0)
    xp, xpb = matmul_residual_ln(op, w_o, xp, g0, b0, 256, "attn_out_ln_prompt")
    xs, xsb = matmul_residual_ln(os_, w_o, xs, g0, b0, ms, "attn_out_ln_sample")
    g1, b1 = ln_params(1, 1)
    wg, wu, wd = (w[1].astype(BF16) for w in (ffn_w_gate, ffn_w_up, ffn_w_down))
    xp, _ = _ffn_block(xp, xpb, wg, wu, wd, g1, b1, 1024, 512, "l1_prompt")
    xs, _ = _ffn_block(xs, xsb, wg, wu, wd, g1, b1, ms, ms, "l1_sample")

    return (xp.reshape(bp, seq, d), xs.reshape(bs, ts_, d),
            conv_p[None], conv_s[None], pool_p[None], pool_s[None],
            kp.reshape(1, bp, seq, N_HEADS, dh), vp.reshape(1, bp, seq, N_HEADS, dh),
            ks4[None], vs4[None])
```

```python
import functools

import jax
import jax.numpy as jnp
from jax import lax
from jax.experimental import pallas as pl
from jax.experimental.pallas import tpu as pltpu

F32 = jnp.float32
BF16 = jnp.bfloat16

CONV_WIDTH = 3
POOL_WINDOWS = (2, 4, 8, 16)
POOL_BUF = max(POOL_WINDOWS) - 1
N_HEADS = 16
PAGE_SIZE = 128
DEPTH = 2
ALPHA = (2.0 * DEPTH) ** 0.25
LN_EPS = 1e-5

VMEM_LIMIT_BYTES = 60 * 1024 * 1024
SUBLANES = 8


def _params(n_axes):
    return pltpu.CompilerParams(
        dimension_semantics=("arbitrary",) * n_axes,
        vmem_limit_bytes=VMEM_LIMIT_BYTES,
    )


def _mm_kernel(x_ref, w_ref, o_ref):
    o_ref[...] = jnp.dot(x_ref[...].astype(BF16), w_ref[...],
                         preferred_element_type=F32).astype(o_ref.dtype)


def matmul(x, w, out_dtype, tm, tn, name):
    m, k = x.shape
    n = w.shape[1]
    return pl.pallas_call(
        _mm_kernel,
        grid=(n // tn, m // tm),
        in_specs=[pl.BlockSpec((tm, k), lambda j, i: (i, 0)),
                  pl.BlockSpec((k, tn), lambda j, i: (0, j))],
        out_specs=pl.BlockSpec((tm, tn), lambda j, i: (i, j)),
        out_shape=jax.ShapeDtypeStruct((m, n), out_dtype),
        compiler_params=_params(2),
        name=name,
    )(x, w)


def _layer_norm_rows(s, g, b):
    mu = jnp.mean(s, axis=-1, keepdims=True)
    d = s - mu
    var = jnp.mean(d * d, axis=-1, keepdims=True)
    return d * lax.rsqrt(var + LN_EPS) * g + b


def _mm_ln_kernel(x_ref, w_ref, r_ref, g_ref, b_ref, y_ref, yb_ref):
    acc = jnp.dot(x_ref[...].astype(BF16), w_ref[...], preferred_element_type=F32)
    y = _layer_norm_rows(ALPHA * r_ref[...] + acc, g_ref[...], b_ref[...])
    y_ref[...] = y
    yb_ref[...] = y.astype(BF16)


def matmul_residual_ln(x, w, resid, g, b, tm, name):
    m, k = x.shape
    d = w.shape[1]
    row = lambda i: (i, 0)
    const = lambda i: (0, 0)
    return pl.pallas_call(
        _mm_ln_kernel,
        grid=(m // tm,),
        in_specs=[pl.BlockSpec((tm, k), row),
                  pl.BlockSpec((k, d), const, pipeline_mode=pl.Buffered(1)),
                  pl.BlockSpec((tm, d), row), pl.BlockSpec((1, d), const), pl.BlockSpec((1, d), const)],
        out_specs=[pl.BlockSpec((tm, d), row), pl.BlockSpec((tm, d), row)],
        out_shape=[jax.ShapeDtypeStruct((m, d), F32), jax.ShapeDtypeStruct((m, d), BF16)],
        compiler_params=_params(1),
        name=name,
    )(x, w, resid, g, b)


def _ffn_up_kernel(x_ref, wg_ref, wu_ref, h_ref):
    x = x_ref[...]
    gate = jnp.dot(x, wg_ref[...], preferred_element_type=F32)
    up = jnp.dot(x, wu_ref[...], preferred_element_type=F32)
    h_ref[...] = (jax.nn.silu(gate) * up).astype(h_ref.dtype)


def ffn_up(x, wg, wu, tm, tf, name):
    m, k = x.shape
    f = wg.shape[1]
    return pl.pallas_call(
        _ffn_up_kernel,
        grid=(f // tf, m // tm),
        in_specs=[pl.BlockSpec((tm, k), lambda j, i: (i, 0)),
                  pl.BlockSpec((k, tf), lambda j, i: (0, j)),
                  pl.BlockSpec((k, tf), lambda j, i: (0, j))],
        out_specs=pl.BlockSpec((tm, tf), lambda j, i: (i, j)),
        out_shape=jax.ShapeDtypeStruct((m, f), BF16),
        compiler_params=_params(2),
        name=name,
    )(x, wg, wu)


def _mixer_prompt_kernel(h_ref, wc_ref, wp_ref, sp_ref, z_ref, co_ref, po_ref, cbuf, pbuf, *, ts, dc):
    i = pl.program_id(1)
    ch, ph = SUBLANES, 2 * SUBLANES
    dp = pbuf.shape[1]
    pg = dp // len(POOL_WINDOWS)

    @pl.when(i == 0)
    def _():
        cbuf[0:ch, :] = jnp.zeros((ch, dc), F32)
        pbuf[0:ph, :] = jnp.zeros((ph, dp), F32)

    gb = h_ref[:, 0:dc]
    u = h_ref[:, dc:2 * dc] * h_ref[:, 2 * dc:3 * dc]
    pv = h_ref[:, 3 * dc:3 * dc + dp]
    cbuf[ch:ch + ts, :] = u
    pbuf[ph:ph + ts, :] = pv

    y = cbuf[ch - 2:ch - 2 + ts, :] * wc_ref[0:1, :]
    y = y + cbuf[ch - 1:ch - 1 + ts, :] * wc_ref[1:2, :]
    y = y + u * wc_ref[2:3, :]
    z_ref[:, 0:dc] = (gb * y).astype(z_ref.dtype)

    pos = i * ts + lax.broadcasted_iota(jnp.int32, (ts, 1), 0)
    for gi, w in enumerate(POOL_WINDOWS):
        lo = gi * pg
        cur = pv[:, lo:lo + pg]
        s = cur
        for back in range(1, w):
            s = s + pbuf[ph - back:ph - back + ts, lo:lo + pg]
        cnt = jnp.minimum(w, pos + 1).astype(F32)
        dlt = s / cnt - cur
        yg = jnp.dot(dlt.astype(BF16), wp_ref[gi], preferred_element_type=F32)
        z_ref[:, dc + lo:dc + lo + pg] = (yg * sp_ref[:, lo:lo + pg]).astype(z_ref.dtype)

    co_ref[...] = cbuf[ch + ts - (CONV_WIDTH - 1):ch + ts, :]
    po_ref[...] = pbuf[ph + ts - POOL_BUF:ph + ts, :]
    cbuf[0:ch, :] = cbuf[ts:ts + ch, :]
    pbuf[0:ph, :] = pbuf[ts:ts + ph, :]


def mixer_prompt(h, wc, wp, sp, batch, seq, ts):
    rows, width = h.shape
    dp = sp.shape[1]
    dc = (width - dp) // 3
    nt = seq // ts
    return pl.pallas_call(
        functools.partial(_mixer_prompt_kernel, ts=ts, dc=dc),
        grid=(batch, nt),
        in_specs=[pl.BlockSpec((ts, width), lambda b, i: (b * nt + i, 0)),
                  pl.BlockSpec(wc.shape, lambda b, i: (0, 0)),
                  pl.BlockSpec(wp.shape, lambda b, i: (0, 0, 0)),
                  pl.BlockSpec(sp.shape, lambda b, i: (0, 0))],
        out_specs=[pl.BlockSpec((ts, dc + dp), lambda b, i: (b * nt + i, 0)),
                   pl.BlockSpec((None, CONV_WIDTH - 1, dc), lambda b, i: (b, 0, 0)),
                   pl.BlockSpec((None, POOL_BUF, dp), lambda b, i: (b, 0, 0))],
        out_shape=[jax.ShapeDtypeStruct((rows, dc + dp), BF16),
                   jax.ShapeDtypeStruct((batch, CONV_WIDTH - 1, dc), F32),
                   jax.ShapeDtypeStruct((batch, POOL_BUF, dp), F32)],
        scratch_shapes=[pltpu.VMEM((ts + SUBLANES, dc), F32), pltpu.VMEM((ts + 2 * SUBLANES, dp), F32)],
        compiler_params=_params(2),
        name="mixer_prompt",
    )(h, wc, wp, sp)


def _mixer_sample_kernel(h_ref, sc_ref, spool_ref, wc_ref, wp_ref, sp_ref, z_ref, co_ref, po_ref,
                         *, nb, nt, dc, pos0):
    dp = sp_ref.shape[1]
    pg = dp // len(POOL_WINDOWS)
    nc = CONV_WIDTH - 1

    conv_ext = [sc_ref[j] for j in range(nc)]
    pool_ext = [spool_ref[j] for j in range(POOL_BUF)]
    gbs = []
    for t in range(nt):
        gbs.append(h_ref[t, :, 0:dc])
        conv_ext.append(h_ref[t, :, dc:2 * dc] * h_ref[t, :, 2 * dc:3 * dc])
        pool_ext.append(h_ref[t, :, 3 * dc:3 * dc + dp])

    for t in range(nt):
        y = conv_ext[t] * wc_ref[0:1, :]
        for kk in range(1, CONV_WIDTH):
            y = y + conv_ext[t + kk] * wc_ref[kk:kk + 1, :]
        z_ref[t, :, 0:dc] = gbs[t] * y
    for j in range(nc):
        co_ref[j] = conv_ext[nt + j]

    for gi, w in enumerate(POOL_WINDOWS):
        lo = gi * pg
        dls = []
        for t in range(nt):
            e = POOL_BUF + t
            s = pool_ext[e][:, lo:lo + pg]
            for back in range(1, w):
                s = s + pool_ext[e - back][:, lo:lo + pg]
            cnt = float(min(w, pos0 + t + 1))
            dls.append(s / cnt - pool_ext[e][:, lo:lo + pg])
        dl = jnp.concatenate(dls, axis=0).astype(BF16)
        yg = jnp.dot(dl, wp_ref[gi], preferred_element_type=F32) * sp_ref[:, lo:lo + pg]
        for t in range(nt):
            z_ref[t, :, dc + lo:dc + lo + pg] = yg[t * nb:(t + 1) * nb, :]
    for j in range(POOL_BUF):
        po_ref[j] = pool_ext[nt + j]


def mixer_sample(h, state_conv, state_pool, wc, wp, sp, nb, nt, pos0):
    rows, width = h.shape
    dp = sp.shape[1]
    dc = (width - dp) // 3
    nc = CONV_WIDTH - 1
    step_major = lambda a: jnp.swapaxes(a, 0, 1)
    z, conv_new, pool_new = pl.pallas_call(
        functools.partial(_mixer_sample_kernel, nb=nb, nt=nt, dc=dc, pos0=pos0),
        out_shape=[jax.ShapeDtypeStruct((nt, nb, dc + dp), F32),
                   jax.ShapeDtypeStruct((nc, nb, dc), F32),
                   jax.ShapeDtypeStruct((POOL_BUF, nb, dp), F32)],
        compiler_params=pltpu.CompilerParams(vmem_limit_bytes=VMEM_LIMIT_BYTES),
        name="mixer_sample",
    )(step_major(h.reshape(nb, nt, width)), step_major(state_conv), step_major(state_pool), wc, wp, sp)
    return step_major(z).reshape(rows, dc + dp), step_major(conv_new), step_major(pool_new)


def _neg_softplus(z):
    return -(jnp.maximum(z, 0.0) + jnp.log1p(jnp.exp(-jnp.abs(z))))


def _suffix_sum(ls, tri):
    hi = ls.astype(BF16)
    lo = (ls - hi.astype(F32)).astype(BF16)
    return (jnp.dot(hi, tri, preferred_element_type=F32)
            + jnp.dot(lo, tri, preferred_element_type=F32))


def _lower_tri(n):
    r = lax.broadcasted_iota(jnp.int32, (n, n), 0)
    c = lax.broadcasted_iota(jnp.int32, (n, n), 1)
    return jnp.where(r >= c, 1.0, 0.0).astype(BF16)


def _sb_prompt_kernel(bias_ref, q_ref, k_ref, v_ref, o_ref, kb_ref, vb_ref, *, tb, hp, scale):
    seq = q_ref.shape[0]
    dh = q_ref.shape[1] // hp
    nblk = seq // tb
    biases = [bias_ref[pl.program_id(1) * hp + a] for a in range(hp)]
    kb_ref[...] = k_ref[...].astype(BF16)
    vb_ref[...] = v_ref[...].astype(BF16)
    tri = _lower_tri(tb)
    r = lax.broadcasted_iota(jnp.int32, (tb, tb), 0)
    c_ = lax.broadcasted_iota(jnp.int32, (tb, tb), 1)
    causal = c_ < r

    def logits(qs, j, masked):
        start = pl.multiple_of(j * tb, tb)
        out = []
        for a in range(hp):
            kj = kb_ref[pl.ds(start, tb), a * dh:(a + 1) * dh]
            z = lax.dot_general(qs[a], kj, (((1,), (1,)), ((), ())), preferred_element_type=F32)
            z = z * scale + biases[a]
            ls = _neg_softplus(z)
            if masked:
                ls = jnp.where(causal, ls, 0.0)
            hi = ls.astype(BF16)
            lo = (ls - hi.astype(F32)).astype(BF16)
            out.append((z, hi, lo))
        return tuple(out)

    def accumulate(j, staged, carry, masked):
        start = pl.multiple_of(j * tb, tb)
        out = []
        for a in range(hp):
            z, hi, lo = staged[a]
            c, o = carry[a]
            vj = vb_ref[pl.ds(start, tb), a * dh:(a + 1) * dh]
            rc = (jnp.dot(hi, tri, preferred_element_type=F32)
                  + jnp.dot(lo, tri, preferred_element_type=F32))
            e = jnp.exp(z + rc + c)
            if masked:
                e = jnp.where(causal, e, 0.0)
            o = o + jnp.dot(e.astype(BF16), vj, preferred_element_type=F32)
            out.append((c + rc[:, 0:1], o))
        return tuple(out)

    for i in range(nblk):
        qs = [q_ref[i * tb:(i + 1) * tb, a * dh:(a + 1) * dh].astype(BF16) for a in range(hp)]
        carry = tuple((jnp.zeros((tb, 1), F32), jnp.zeros((tb, dh), F32)) for _ in range(hp))
        staged = logits(qs, i, True)
        if i > 0:
            nxt = logits(qs, i - 1, False)
            carry = accumulate(i, staged, carry, True)
            staged = nxt

            def trip(n, state):
                st, cr = state
                nx = logits(qs, i - 1 - n, False)
                return nx, accumulate(i - n, st, cr, False)

            staged, carry = lax.fori_loop(1, i, trip, (staged, carry))
            carry = accumulate(0, staged, carry, False)
        else:
            carry = accumulate(0, staged, carry, True)
        for a in range(hp):
            o_ref[i * tb:(i + 1) * tb, a * dh:(a + 1) * dh] = carry[a][1].astype(o_ref.dtype)


def sb_attention_prompt(q, k, v, bias, batch, seq, tb, hp):
    rows, d_attn = q.shape
    dh = d_attn // N_HEADS
    blk = pl.BlockSpec((seq, hp * dh), lambda b, h: (b, h))
    return pl.pallas_call(
        functools.partial(_sb_prompt_kernel, tb=tb, hp=hp, scale=dh ** -0.5),
        grid=(batch, N_HEADS // hp),
        in_specs=[pl.BlockSpec(memory_space=pltpu.SMEM), blk, blk, blk],
        out_specs=blk,
        out_shape=jax.ShapeDtypeStruct((rows, d_attn), BF16),
        scratch_shapes=[pltpu.VMEM((seq, hp * dh), BF16), pltpu.VMEM((seq, hp * dh), BF16)],
        compiler_params=_params(2),
        name="sb_attention_prompt",
    )(bias, q, k, v)


def _sb_sample_kernel(pt_ref, q_ref, kn_ref, vn_ref, bias_ref, *refs, n_pg, scale):
    del pt_ref
    nt, nh, dh = kn_ref.shape
    ngrp = nh // SUBLANES
    k_refs = refs[:n_pg * ngrp]
    v_refs = refs[n_pg * ngrp:2 * n_pg * ngrp]
    o_ref, qf_ref, c_ref, acc_ref = refs[2 * n_pg * ngrp:]

    def head_rows(page_refs, p, h):
        flat = page_refs[p * ngrp + h // SUBLANES].reshape(PAGE_SIZE * SUBLANES, dh)
        return flat[pl.ds(h % SUBLANES, PAGE_SIZE, stride=SUBLANES), :]

    rpad = qf_ref.shape[0] // nh
    g = pl.program_id(1)
    bias_col = bias_ref[...]

    @pl.when(g == 0)
    def _():
        for h in range(nh):
            qf_ref[h * rpad:(h + 1) * rpad, :] = q_ref[:, h * dh:(h + 1) * dh]
        rr = lax.broadcasted_iota(jnp.int32, (nh * rpad, nh), 0)
        hh = lax.broadcasted_iota(jnp.int32, (nh * rpad, nh), 1)
        rep = jnp.where(rr // rpad == hh, 1.0, 0.0).astype(BF16)
        step = lax.broadcasted_iota(jnp.int32, (nh * rpad, 1), 0) % rpad
        qr = qf_ref[...].astype(BF16).astype(F32)
        zs, lss, valids = [], [], []
        for s in range(nt):
            kr = jnp.dot(rep, kn_ref[s].astype(BF16), preferred_element_type=F32)
            z = jnp.sum(qr * kr, axis=1, keepdims=True) * scale + bias_col
            valid = step > s
            zs.append(z)
            valids.append(valid)
            lss.append(jnp.where(valid, _neg_softplus(z), 0.0))
        acc = jnp.zeros(acc_ref.shape, F32)
        rc = jnp.zeros((nh * rpad, 1), F32)
        for s in reversed(range(nt)):
            rc = rc + lss[s]
            a = jnp.where(valids[s], jnp.exp(zs[s] + rc), 0.0)
            vr = jnp.dot(rep, vn_ref[s].astype(BF16), preferred_element_type=F32)
            acc = acc + a.astype(BF16).astype(F32) * vr
        acc_ref[...] = acc
        c_ref[...] = rc

    width = n_pg * PAGE_SIZE
    zrows = []
    for h in range(nh):
        qh = qf_ref[h * rpad:(h + 1) * rpad, :].astype(BF16)
        kh = jnp.concatenate([head_rows(k_refs, p, h) for p in range(n_pg)], axis=0).astype(BF16)
        zrows.append(lax.dot_general(qh, kh, (((1,), (1,)), ((), ())), preferred_element_type=F32))
    z = jnp.concatenate(zrows, axis=0) * scale + bias_col
    ls = _neg_softplus(z)
    tb = 2 * PAGE_SIZE
    tri = _lower_tri(tb)
    c = c_ref[...]
    a_blocks = [None] * (width // tb)
    for blk in reversed(range(width // tb)):
        rc = _suffix_sum(ls[:, blk * tb:(blk + 1) * tb], tri)
        a_blocks[blk] = jnp.exp(z[:, blk * tb:(blk + 1) * tb] + rc + c).astype(BF16)
        c = c + rc[:, 0:1]
    c_ref[...] = c
    a = jnp.concatenate(a_blocks, axis=1)
    orows = []
    for h in range(nh):
        vh = jnp.concatenate([head_rows(v_refs, p, h) for p in range(n_pg)], axis=0).astype(BF16)
        orows.append(jnp.dot(a[h * rpad:(h + 1) * rpad, :], vh, preferred_element_type=F32))
    acc_ref[...] += jnp.concatenate(orows, axis=0)

    @pl.when(g == pl.num_programs(1) - 1)
    def _():
        for h in range(nh):
            o_ref[:, h * dh:(h + 1) * dh] = acc_ref[h * rpad:(h + 1) * rpad, :]


def sb_attention_sample(q, k_new, v_new, bias, cache_k, cache_v, page_table, n_pg):
    nb, nt, d_attn = q.shape
    nh, dh = k_new.shape[2], k_new.shape[3]
    n_pages = page_table.shape[1]
    rpad = SUBLANES
    q_pad = jnp.pad(q, ((0, 0), (0, rpad - nt), (0, 0)))
    bias_col = jnp.repeat(bias.astype(F32), rpad).reshape(nh * rpad, 1)
    n_steps = n_pages // n_pg

    ngrp = nh // SUBLANES
    cache_k = cache_k.reshape(-1, PAGE_SIZE, ngrp, SUBLANES, dh)
    cache_v = cache_v.reshape(-1, PAGE_SIZE, ngrp, SUBLANES, dh)

    def page_spec(p, grp):
        return pl.BlockSpec(
            (None, PAGE_SIZE, None, SUBLANES, dh),
            lambda b, g, pt: (pt[b, n_pages - (g + 1) * n_pg + p], 0, grp, 0, 0))

    page_specs = [page_spec(p, grp) for p in range(n_pg) for grp in range(ngrp)]

    per_seq3 = lambda b, g, pt: (b, 0, 0)
    per_seq4 = lambda b, g, pt: (b, 0, 0, 0)
    grid_spec = pltpu.PrefetchScalarGridSpec(
        num_scalar_prefetch=1,
        grid=(nb, n_steps),
        in_specs=[pl.BlockSpec((None, rpad, d_attn), per_seq3),
                  pl.BlockSpec((None, nt, nh, dh), per_seq4),
                  pl.BlockSpec((None, nt, nh, dh), per_seq4),
                  pl.BlockSpec((nh * rpad, 1), lambda b, g, pt: (0, 0))]
                 + page_specs * 2,
        out_specs=pl.BlockSpec((None, rpad, d_attn), per_seq3),
        scratch_shapes=[pltpu.VMEM((nh * rpad, dh), F32), pltpu.VMEM((nh * rpad, 1), F32),
                        pltpu.VMEM((nh * rpad, dh), F32)],
    )
    return pl.pallas_call(
        functools.partial(_sb_sample_kernel, n_pg=n_pg, scale=dh ** -0.5),
        grid_spec=grid_spec,
        out_shape=jax.ShapeDtypeStruct((nb, rpad, d_attn), F32),
        compiler_params=_params(2),
        name="sb_attention_sample",
    )(page_table, q_pad, k_new, v_new, bias_col,
      *([cache_k] * len(page_specs)), *([cache_v] * len(page_specs)))


def _ffn_block(x, xb, wg, wu, wd, g, b, tm_up, tm_down, tag):
    h = ffn_up(xb, wg, wu, tm_up, 512, "ffn_up_" + tag)
    return matmul_residual_ln(h, wd, x, g, b, tm_down, "ffn_down_ln_" + tag)


def kernel(x_prompt, x_sample, state_conv, state_pool, cache_k, cache_v, page_table, ln_g, ln_b, mix_w_in, conv_w, pool_w, pool_scale, mix_w_out, attn_w_qkv, attn_w_o, attn_bias, ffn_w_gate, ffn_w_up, ffn_w_down):
    bp, seq, d = x_prompt.shape
    bs, ts_, _ = x_sample.shape
    mp, ms = bp * seq, bs * ts_
    past_len = page_table.shape[1] * PAGE_SIZE
    xp = x_prompt.reshape(mp, d)
    xs = x_sample.reshape(ms, d)
    dp = pool_scale.shape[1]
    dc = conv_w.shape[2]
    d_attn = attn_w_o.shape[1]
    dh = d_attn // N_HEADS

    def ln_params(layer, idx):
        return ln_g[layer, idx].reshape(1, d), ln_b[layer, idx].reshape(1, d)

    w_in = mix_w_in[0].astype(BF16)
    w_out = mix_w_out[0].astype(BF16)
    wp = pool_w[0].astype(BF16)
    sp = pool_scale[0].reshape(1, dp)
    hp = matmul(xp, w_in, F32, 512, 1024, "mix_in_prompt")
    hs = matmul(xs, w_in, F32, ms, 1024, "mix_in_sample")
    zp, conv_p, pool_p = mixer_prompt(hp, conv_w[0], wp, sp, bp, seq, 256)
    zs, conv_s, pool_s = mixer_sample(hs, state_conv[0], state_pool[0], conv_w[0], wp, sp, bs, ts_, past_len)
    g0, b0 = ln_params(0, 0)
    xp, xpb = matmul_residual_ln(zp, w_out, xp, g0, b0, 256, "mix_out_ln_prompt")
    xs, xsb = matmul_residual_ln(zs, w_out, xs, g0, b0, ms, "mix_out_ln_sample")
    g1, b1 = ln_params(0, 1)
    wg, wu, wd = (w[0].astype(BF16) for w in (ffn_w_gate, ffn_w_up, ffn_w_down))
    xp, xpb = _ffn_block(xp, xpb, wg, wu, wd, g1, b1, 1024, 256, "l0_prompt")
    xs, xsb = _ffn_block(xs, xsb, wg, wu, wd, g1, b1, ms, ms, "l0_sample")

    w_q, w_k, w_v = (attn_w_qkv[0][:, i * d_attn:(i + 1) * d_attn].astype(BF16) for i in range(3))
    w_o = attn_w_o[0].astype(BF16)
    bias = attn_bias[0]
    qp, kp, vp = (matmul(xpb, w, F32, 1024, 1024, "qkv_prompt") for w in (w_q, w_k, w_v))
    qs, ks, vs = (matmul(xsb, w, F32, ms, 1024, "qkv_sample") for w in (w_q, w_k, w_v))
    op = sb_attention_prompt(qp, kp, vp, bias, bp, seq, 256, 2)
    ks4 = ks.reshape(bs, ts_, N_HEADS, dh)
    vs4 = vs.reshape(bs, ts_, N_HEADS, dh)
    o_pad = sb_attention_sample(qs.reshape(bs, ts_, d_attn), ks4, vs4, bias, cache_k[0], cache_v[0],
                                page_table, 8)
    os_ = o_pad[:, :ts_, :].reshape(ms, d_attn)
    g0, b0 = ln_params(1, 0)
    xp, xpb = matmul_residual_ln(op, w_o, xp, g0, b0, 256, "attn_out_ln_prompt")
    xs, xsb = matmul_residual_ln(os_, w_o, xs, g0, b0, ms, "attn_out_ln_sample")
    g1, b1 = ln_params(1, 1)
    wg, wu, wd = (w[1].astype(BF16) for w in (ffn_w_gate, ffn_w_up, ffn_w_down))
    xp, _ = _ffn_block(xp, xpb, wg, wu, wd, g1, b1, 1024, 256, "l1_prompt")
    xs, _ = _ffn_block(xs, xsb, wg, wu, wd, g1, b1, ms, ms, "l1_sample")

    return (xp.reshape(bp, seq, d), xs.reshape(bs, ts_, d),
            conv_p[None], conv_s[None], pool_p[None], pool_s[None],
            kp.reshape(1, bp, seq, N_HEADS, dh), vp.reshape(1, bp, seq, N_HEADS, dh),
            ks4[None], vs4[None])
```

```python
import functools

import jax
import jax.numpy as jnp
from jax import lax
from jax.experimental import pallas as pl
from jax.experimental.pallas import tpu as pltpu

F32 = jnp.float32
BF16 = jnp.bfloat16

CONV_WIDTH = 3
POOL_WINDOWS = (2, 4, 8, 16)
POOL_BUF = max(POOL_WINDOWS) - 1
N_HEADS = 16
PAGE_SIZE = 128
DEPTH = 2
ALPHA = (2.0 * DEPTH) ** 0.25
LN_EPS = 1e-5

VMEM_LIMIT_BYTES = 60 * 1024 * 1024
SUBLANES = 8


def _params(n_axes):
    return pltpu.CompilerParams(
        dimension_semantics=("arbitrary",) * n_axes,
        vmem_limit_bytes=VMEM_LIMIT_BYTES,
    )


def _dot(x, w):
    return jnp.dot(x.astype(BF16), w, preferred_element_type=F32)


def _mm_kernel(xm_ref, xe_ref, w_ref, om_ref, oe_ref, wb_ref):
    @pl.when(pl.program_id(1) == 0)
    def _():
        wb_ref[...] = w_ref[...].astype(BF16)
        oe_ref[...] = _dot(xe_ref[...], wb_ref[...]).astype(oe_ref.dtype)

    om_ref[...] = _dot(xm_ref[...], wb_ref[...]).astype(om_ref.dtype)


def matmul(xm, xe, w, layer, col0, n, out_dtype, tm, tn, name):
    m, k = xm.shape
    me = xe.shape[0]
    cb0 = col0 // tn
    return pl.pallas_call(
        _mm_kernel,
        grid=(n // tn, m // tm),
        in_specs=[pl.BlockSpec((tm, k), lambda j, i: (i, 0)),
                  pl.BlockSpec((me, k), lambda j, i: (0, 0)),
                  pl.BlockSpec((None, k, tn), lambda j, i: (layer, 0, cb0 + j))],
        out_specs=[pl.BlockSpec((tm, tn), lambda j, i: (i, j)),
                   pl.BlockSpec((me, tn), lambda j, i: (0, j))],
        out_shape=[jax.ShapeDtypeStruct((m, n), out_dtype), jax.ShapeDtypeStruct((me, n), out_dtype)],
        scratch_shapes=[pltpu.VMEM((k, tn), BF16)],
        compiler_params=_params(2),
        name=name,
    )(xm, xe, w)


def _layer_norm_rows(s, g, b):
    mu = jnp.mean(s, axis=-1, keepdims=True)
    d = s - mu
    var = jnp.mean(d * d, axis=-1, keepdims=True)
    return d * lax.rsqrt(var + LN_EPS) * g + b


def _mm_ln_kernel(xm_ref, xe_ref, w_ref, rm_ref, re_ref, g_ref, b_ref,
                  ym_ref, ymb_ref, ye_ref, yeb_ref, *wb_refs):
    wb_ref = wb_refs[0] if wb_refs else w_ref

    def rows(x_ref, r_ref, y_ref, yb_ref):
        y = _layer_norm_rows(ALPHA * r_ref[...] + _dot(x_ref[...], wb_ref[...]), g_ref[...], b_ref[...])
        y_ref[...] = y
        yb_ref[...] = y.astype(BF16)

    @pl.when(pl.program_id(0) == 0)
    def _():
        if wb_refs:
            wb_ref[...] = w_ref[...].astype(BF16)
        rows(xe_ref, re_ref, ye_ref, yeb_ref)

    rows(xm_ref, rm_ref, ym_ref, ymb_ref)


def matmul_residual_ln(xm, xe, w, layer, rm, re, ln_g, ln_b, ln_row, tm, name):
    m, k = xm.shape
    me = xe.shape[0]
    d = w.shape[-1]
    row = lambda i: (i, 0)
    const = lambda i: (0, 0)
    if w.ndim == 2:
        w_spec = pl.BlockSpec((k, d), const, pipeline_mode=pl.Buffered(1))
        scratch = []
    else:
        w_spec = pl.BlockSpec((None, k, d), lambda i: (layer, 0, 0), pipeline_mode=pl.Buffered(1))
        scratch = [pltpu.VMEM((k, d), BF16)]
    ln_spec = pl.BlockSpec((None, 1, d), lambda i: (ln_row, 0, 0))
    return pl.pallas_call(
        _mm_ln_kernel,
        grid=(m // tm,),
        in_specs=[pl.BlockSpec((tm, k), row), pl.BlockSpec((me, k), const), w_spec,
                  pl.BlockSpec((tm, d), row), pl.BlockSpec((me, d), const), ln_spec, ln_spec],
        out_specs=[pl.BlockSpec((tm, d), row), pl.BlockSpec((tm, d), row),
                   pl.BlockSpec((me, d), const), pl.BlockSpec((me, d), const)],
        out_shape=[jax.ShapeDtypeStruct((m, d), F32), jax.ShapeDtypeStruct((m, d), BF16),
                   jax.ShapeDtypeStruct((me, d), F32), jax.ShapeDtypeStruct((me, d), BF16)],
        scratch_shapes=scratch,
        compiler_params=_params(1),
        name=name,
    )(xm, xe, w, rm, re, ln_g, ln_b)


def _ffn_up_kernel(xm_ref, xe_ref, wg_ref, wu_ref, wd_ref, hm_ref, he_ref, wdb_ref, wgb_ref, wub_ref):
    def rows(x_ref, h_ref):
        x = x_ref[...]
        h_ref[...] = (jax.nn.silu(_dot(x, wgb_ref[...])) * _dot(x, wub_ref[...])).astype(h_ref.dtype)

    @pl.when(pl.program_id(1) == 0)
    def _():
        wgb_ref[...] = wg_ref[...].astype(BF16)
        wub_ref[...] = wu_ref[...].astype(BF16)
        wdb_ref[...] = wd_ref[...].astype(BF16)
        rows(xe_ref, he_ref)

    rows(xm_ref, hm_ref)


def ffn_up(xm, xe, wg, wu, wd, layer, tm, tf, name):
    m, k = xm.shape
    me = xe.shape[0]
    f = wg.shape[2]
    d = wd.shape[2]
    col = lambda j, i: (layer, 0, j)
    return pl.pallas_call(
        _ffn_up_kernel,
        grid=(f // tf, m // tm),
        in_specs=[pl.BlockSpec((tm, k), lambda j, i: (i, 0)),
                  pl.BlockSpec((me, k), lambda j, i: (0, 0)),
                  pl.BlockSpec((None, k, tf), col), pl.BlockSpec((None, k, tf), col),
                  pl.BlockSpec((None, tf, d), lambda j, i: (layer, j, 0))],
        out_specs=[pl.BlockSpec((tm, tf), lambda j, i: (i, j)),
                   pl.BlockSpec((me, tf), lambda j, i: (0, j)),
                   pl.BlockSpec((tf, d), lambda j, i: (j, 0))],
        out_shape=[jax.ShapeDtypeStruct((m, f), BF16), jax.ShapeDtypeStruct((me, f), BF16),
                   jax.ShapeDtypeStruct((f, d), BF16)],
        scratch_shapes=[pltpu.VMEM((k, tf), BF16), pltpu.VMEM((k, tf), BF16)],
        compiler_params=_params(2),
        name=name,
    )(xm, xe, wg, wu, wd)


def _mixer_prompt_kernel(h_ref, wc_ref, wp_ref, sp_ref, z_ref, co_ref, po_ref, cbuf, pbuf, *, ts, dc):
    i = pl.program_id(1)
    ch, ph = SUBLANES, 2 * SUBLANES
    dp = pbuf.shape[1]
    pg = dp // len(POOL_WINDOWS)

    @pl.when(i == 0)
    def _():
        cbuf[0:ch, :] = jnp.zeros((ch, dc), F32)
        pbuf[0:ph, :] = jnp.zeros((ph, dp), F32)

    gb = h_ref[:, 0:dc]
    u = h_ref[:, dc:2 * dc] * h_ref[:, 2 * dc:3 * dc]
    pv = h_ref[:, 3 * dc:3 * dc + dp]
    cbuf[ch:ch + ts, :] = u
    pbuf[ph:ph + ts, :] = pv

    y = cbuf[ch - 2:ch - 2 + ts, :] * wc_ref[0:1, :]
    y = y + cbuf[ch - 1:ch - 1 + ts, :] * wc_ref[1:2, :]
    y = y + u * wc_ref[2:3, :]
    z_ref[:, 0:dc] = (gb * y).astype(z_ref.dtype)

    pos = i * ts + lax.broadcasted_iota(jnp.int32, (ts, 1), 0)
    for gi, w in enumerate(POOL_WINDOWS):
        lo = gi * pg
        cur = pv[:, lo:lo + pg]
        s = cur
        for back in range(1, w):
            s = s + pbuf[ph - back:ph - back + ts, lo:lo + pg]
        cnt = jnp.minimum(w, pos + 1).astype(F32)
        dlt = s / cnt - cur
        yg = jnp.dot(dlt.astype(BF16), wp_ref[gi], preferred_element_type=F32)
        z_ref[:, dc + lo:dc + lo + pg] = (yg * sp_ref[:, lo:lo + pg]).astype(z_ref.dtype)

    co_ref[...] = cbuf[ch + ts - (CONV_WIDTH - 1):ch + ts, :]
    po_ref[...] = pbuf[ph + ts - POOL_BUF:ph + ts, :]
    cbuf[0:ch, :] = cbuf[ts:ts + ch, :]
    pbuf[0:ph, :] = pbuf[ts:ts + ph, :]


def mixer_prompt(h, wc, wp, sp, batch, seq, ts):
    rows, width = h.shape
    dp = sp.shape[1]
    dc = (width - dp) // 3
    nt = seq // ts
    return pl.pallas_call(
        functools.partial(_mixer_prompt_kernel, ts=ts, dc=dc),
        grid=(batch, nt),
        in_specs=[pl.BlockSpec((ts, width), lambda b, i: (b * nt + i, 0)),
                  pl.BlockSpec(wc.shape, lambda b, i: (0, 0)),
                  pl.BlockSpec(wp.shape, lambda b, i: (0, 0, 0)),
                  pl.BlockSpec(sp.shape, lambda b, i: (0, 0))],
        out_specs=[pl.BlockSpec((ts, dc + dp), lambda b, i: (b * nt + i, 0)),
                   pl.BlockSpec((None, CONV_WIDTH - 1, dc), lambda b, i: (b, 0, 0)),
                   pl.BlockSpec((None, POOL_BUF, dp), lambda b, i: (b, 0, 0))],
        out_shape=[jax.ShapeDtypeStruct((rows, dc + dp), BF16),
                   jax.ShapeDtypeStruct((batch, CONV_WIDTH - 1, dc), F32),
                   jax.ShapeDtypeStruct((batch, POOL_BUF, dp), F32)],
        scratch_shapes=[pltpu.VMEM((ts + SUBLANES, dc), F32), pltpu.VMEM((ts + 2 * SUBLANES, dp), F32)],
        compiler_params=_params(2),
        name="mixer_prompt",
    )(h, wc, wp, sp)


def _mixer_sample_kernel(h_ref, sc_ref, spool_ref, wc_ref, wp_ref, sp_ref, z_ref, co_ref, po_ref,
                         *, nb, nt, dc, pos0):
    dp = sp_ref.shape[1]
    pg = dp // len(POOL_WINDOWS)
    nc = CONV_WIDTH - 1

    conv_ext = [sc_ref[j] for j in range(nc)]
    pool_ext = [spool_ref[j] for j in range(POOL_BUF)]
    gbs = []
    for t in range(nt):
        gbs.append(h_ref[t, :, 0:dc])
        conv_ext.append(h_ref[t, :, dc:2 * dc] * h_ref[t, :, 2 * dc:3 * dc])
        pool_ext.append(h_ref[t, :, 3 * dc:3 * dc + dp])

    for t in range(nt):
        y = conv_ext[t] * wc_ref[0:1, :]
        for kk in range(1, CONV_WIDTH):
            y = y + conv_ext[t + kk] * wc_ref[kk:kk + 1, :]
        z_ref[t, :, 0:dc] = gbs[t] * y
    for j in range(nc):
        co_ref[j] = conv_ext[nt + j]

    for gi, w in enumerate(POOL_WINDOWS):
        lo = gi * pg
        dls = []
        for t in range(nt):
            e = POOL_BUF + t
            s = pool_ext[e][:, lo:lo + pg]
            for back in range(1, w):
                s = s + pool_ext[e - back][:, lo:lo + pg]
            cnt = float(min(w, pos0 + t + 1))
            dls.append(s / cnt - pool_ext[e][:, lo:lo + pg])
        dl = jnp.concatenate(dls, axis=0).astype(BF16)
        yg = jnp.dot(dl, wp_ref[gi], preferred_element_type=F32) * sp_ref[:, lo:lo + pg]
        for t in range(nt):
            z_ref[t, :, dc + lo:dc + lo + pg] = yg[t * nb:(t + 1) * nb, :]
    for j in range(POOL_BUF):
        po_ref[j] = pool_ext[nt + j]


def mixer_sample(h, state_conv, state_pool, wc, wp, sp, nb, nt, pos0):
    rows, width = h.shape
    dp = sp.shape[1]
    dc = (width - dp) // 3
    nc = CONV_WIDTH - 1
    step_major = lambda a: jnp.swapaxes(a, 0, 1)
    z, conv_new, pool_new = pl.pallas_call(
        functools.partial(_mixer_sample_kernel, nb=nb, nt=nt, dc=dc, pos0=pos0),
        out_shape=[jax.ShapeDtypeStruct((nt, nb, dc + dp), F32),
                   jax.ShapeDtypeStruct((nc, nb, dc), F32),
                   jax.ShapeDtypeStruct((POOL_BUF, nb, dp), F32)],
        compiler_params=pltpu.CompilerParams(vmem_limit_bytes=VMEM_LIMIT_BYTES),
        name="mixer_sample",
    )(step_major(h.reshape(nb, nt, width)), step_major(state_conv), step_major(state_pool), wc, wp, sp)
    return step_major(z).reshape(rows, dc + dp), step_major(conv_new), step_major(pool_new)


def _neg_softplus(z):
    return -(jnp.maximum(z, 0.0) + jnp.log1p(jnp.exp(-jnp.abs(z))))


def _suffix_sum(ls, tri):
    hi = ls.astype(BF16)
    lo = (ls - hi.astype(F32)).astype(BF16)
    return (jnp.dot(hi, tri, preferred_element_type=F32)
            + jnp.dot(lo, tri, preferred_element_type=F32))


def _lower_tri(n):
    r = lax.broadcasted_iota(jnp.int32, (n, n), 0)
    c = lax.broadcasted_iota(jnp.int32, (n, n), 1)
    return jnp.where(r >= c, 1.0, 0.0).astype(BF16)


def _sb_prompt_kernel(ij_ref, bias_ref, q_ref, k_ref, v_ref, o_ref, qb_ref, kb_ref, vb_ref,
                      zr_ref, z_ref, hi_ref, lo_ref, e_ref, c_ref, acc_ref, *, tb, scale):
    npairs = ij_ref.shape[1]
    bias = bias_ref[pl.program_id(1)]
    qb_ref[...] = q_ref[...].astype(BF16)
    kb_ref[...] = k_ref[...].astype(BF16)
    vb_ref[...] = v_ref[...].astype(BF16)
    for ref in (zr_ref, z_ref, hi_ref, lo_ref, e_ref, c_ref, acc_ref):
        ref[...] = jnp.zeros(ref.shape, ref.dtype)
    tri = _lower_tri(tb)
    key_minus_query = (lax.broadcasted_iota(jnp.int32, (tb, tb), 1)
                       - lax.broadcasted_iota(jnp.int32, (tb, tb), 0))

    def pair(p):
        p = jnp.clip(p, 0, npairs - 1)
        return ij_ref[0, p], ij_ref[1, p]

    def block_rows(ref, blk):
        return ref[pl.ds(pl.multiple_of(blk * tb, tb), tb), :]

    def visible(i, j):
        return key_minus_query < jnp.where(i == j, 0, tb)

    def step(n, s):
        t = 1 - s
        i, j = pair(n)
        zr_ref[s] = lax.dot_general(block_rows(qb_ref, i), block_rows(kb_ref, j),
                                    (((1,), (1,)), ((), ())), preferred_element_type=F32)

        i, j = pair(n - 1)
        z = zr_ref[t] * scale + bias
        ls = -(jnp.maximum(z, 0.0) + jnp.log(1.0 + jnp.exp(-jnp.abs(z))))
        ls = jnp.where(visible(i, j), ls, 0.0)
        hi = ls.astype(BF16)
        z_ref[t] = z
        hi_ref[t] = hi
        lo_ref[t] = (ls - hi.astype(F32)).astype(BF16)

        i, j = pair(n - 2)
        rc = (jnp.dot(hi_ref[s], tri, preferred_element_type=F32)
              + jnp.dot(lo_ref[s], tri, preferred_element_type=F32))
        c = jnp.where(i == j, 0.0, c_ref[...])
        e = jnp.exp(z_ref[s] + rc + c)
        e_ref[s] = jnp.where(visible(i, j), e, 0.0).astype(BF16)
        c_ref[...] = c + rc[:, 0:1]

        i, j = pair(n - 3)
        acc = jnp.where(i == j, 0.0, acc_ref[...])
        new = acc + jnp.dot(e_ref[t], block_rows(vb_ref, j), preferred_element_type=F32)
        acc = jnp.where(n - 3 < npairs, new, acc)
        acc_ref[...] = acc
        o_ref[pl.ds(pl.multiple_of(i * tb, tb), tb), :] = acc.astype(o_ref.dtype)

    def two_steps(m, carry):
        step(2 * m, 0)
        step(2 * m + 1, 1)
        return carry

    n_steps = npairs + 3
    lax.fori_loop(0, (n_steps + 1) // 2, two_steps, 0)


def sb_attention_prompt(q, k, v, bias, batch, seq, tb):
    rows, d_attn = q.shape
    dh = d_attn // N_HEADS
    nblk = seq // tb
    pairs = [(i, j) for i in range(nblk) for j in range(i, -1, -1)]
    ij = jnp.asarray(pairs, dtype=jnp.int32).T
    blk = pl.BlockSpec((seq, dh), lambda b, h: (b, h))
    smem = pl.BlockSpec(memory_space=pltpu.SMEM)
    stage_f32 = pltpu.VMEM((2, tb, tb), F32)
    stage_bf16 = pltpu.VMEM((2, tb, tb), BF16)
    return pl.pallas_call(
        functools.partial(_sb_prompt_kernel, tb=tb, scale=dh ** -0.5),
        grid=(batch, N_HEADS),
        in_specs=[smem, smem, blk, blk, blk],
        out_specs=blk,
        out_shape=jax.ShapeDtypeStruct((rows, d_attn), BF16),
        scratch_shapes=[pltpu.VMEM((seq, dh), BF16)] * 3
                       + [stage_f32, stage_f32, stage_bf16, stage_bf16, stage_bf16,
                          pltpu.VMEM((tb, 1), F32), pltpu.VMEM((tb, dh), F32)],
        compiler_params=_params(2),
        name="sb_attention_prompt",
    )(ij, bias, q, k, v)


def _sb_sample_kernel(pt_ref, q_ref, kn_ref, vn_ref, bias_ref, *refs, n_pg, scale):
    del pt_ref
    nt, nh, dh = kn_ref.shape
    ngrp = nh // SUBLANES
    k_refs = refs[:n_pg * ngrp]
    v_refs = refs[n_pg * ngrp:2 * n_pg * ngrp]
    o_ref, qf_ref, c_ref, acc_ref = refs[2 * n_pg * ngrp:]

    def head_rows(page_refs, p, h):
        flat = page_refs[p * ngrp + h // SUBLANES].reshape(PAGE_SIZE * SUBLANES, dh)
        return flat[pl.ds(h % SUBLANES, PAGE_SIZE, stride=SUBLANES), :]

    rpad = qf_ref.shape[0] // nh
    g = pl.program_id(1)
    bias_col = bias_ref[...]

    @pl.when(g == 0)
    def _():
        for h in range(nh):
            qf_ref[h * rpad:(h + 1) * rpad, :] = q_ref[:, h * dh:(h + 1) * dh]
        rr = lax.broadcasted_iota(jnp.int32, (nh * rpad, nh), 0)
        hh = lax.broadcasted_iota(jnp.int32, (nh * rpad, nh), 1)
        rep = jnp.where(rr // rpad == hh, 1.0, 0.0).astype(BF16)
        step = lax.broadcasted_iota(jnp.int32, (nh * rpad, 1), 0) % rpad
        qr = qf_ref[...].astype(BF16).astype(F32)
        zs, lss, valids = [], [], []
        for s in range(nt):
            kr = jnp.dot(rep, kn_ref[s].astype(BF16), preferred_element_type=F32)
            z = jnp.sum(qr * kr, axis=1, keepdims=True) * scale + bias_col
            valid = step > s
            zs.append(z)
            valids.append(valid)
            lss.append(jnp.where(valid, _neg_softplus(z), 0.0))
        acc = jnp.zeros(acc_ref.shape, F32)
        rc = jnp.zeros((nh * rpad, 1), F32)
        for s in reversed(range(nt)):
            rc = rc + lss[s]
            a = jnp.where(valids[s], jnp.exp(zs[s] + rc), 0.0)
            vr = jnp.dot(rep, vn_ref[s].astype(BF16), preferred_element_type=F32)
            acc = acc + a.astype(BF16).astype(F32) * vr
        acc_ref[...] = acc
        c_ref[...] = rc

    width = n_pg * PAGE_SIZE
    zrows = []
    for h in range(nh):
        qh = qf_ref[h * rpad:(h + 1) * rpad, :].astype(BF16)
        kh = jnp.concatenate([head_rows(k_refs, p, h) for p in range(n_pg)], axis=0).astype(BF16)
        zrows.append(lax.dot_general(qh, kh, (((1,), (1,)), ((), ())), preferred_element_type=F32))
    z = jnp.concatenate(zrows, axis=0) * scale + bias_col
    ls = _neg_softplus(z)
    tb = 2 * PAGE_SIZE
    tri = _lower_tri(tb)
    c = c_ref[...]
    a_blocks = [None] * (width // tb)
    for blk in reversed(range(width // tb)):
        rc = _suffix_sum(ls[:, blk * tb:(blk + 1) * tb], tri)
        a_blocks[blk] = jnp.exp(z[:, blk * tb:(blk + 1) * tb] + rc + c).astype(BF16)
        c = c + rc[:, 0:1]
    c_ref[...] = c
    a = jnp.concatenate(a_blocks, axis=1)
    orows = []
    for h in range(nh):
        vh = jnp.concatenate([head_rows(v_refs, p, h) for p in range(n_pg)], axis=0).astype(BF16)
        orows.append(jnp.dot(a[h * rpad:(h + 1) * rpad, :], vh, preferred_element_type=F32))
    acc_ref[...] += jnp.concatenate(orows, axis=0)

    @pl.when(g == pl.num_programs(1) - 1)
    def _():
        for h in range(nh):
            o_ref[:, h * dh:(h + 1) * dh] = acc_ref[h * rpad:(h + 1) * rpad, :]


def sb_attention_sample(q, k_new, v_new, bias, cache_k, cache_v, page_table, n_pg):
    nb, nt, d_attn = q.shape
    nh, dh = k_new.shape[2], k_new.shape[3]
    n_pages = page_table.shape[1]
    rpad = SUBLANES
    q_pad = jnp.pad(q, ((0, 0), (0, rpad - nt), (0, 0)))
    bias_col = jnp.repeat(bias.astype(F32), rpad).reshape(nh * rpad, 1)
    n_steps = n_pages // n_pg

    ngrp = nh // SUBLANES
    cache_k = cache_k.reshape(-1, PAGE_SIZE, ngrp, SUBLANES, dh)
    cache_v = cache_v.reshape(-1, PAGE_SIZE, ngrp, SUBLANES, dh)

    def page_spec(p, grp):
        return pl.BlockSpec(
            (None, PAGE_SIZE, None, SUBLANES, dh),
            lambda b, g, pt: (pt[b, n_pages - (g + 1) * n_pg + p], 0, grp, 0, 0))

    page_specs = [page_spec(p, grp) for p in range(n_pg) for grp in range(ngrp)]

    per_seq3 = lambda b, g, pt: (b, 0, 0)
    per_seq4 = lambda b, g, pt: (b, 0, 0, 0)
    grid_spec = pltpu.PrefetchScalarGridSpec(
        num_scalar_prefetch=1,
        grid=(nb, n_steps),
        in_specs=[pl.BlockSpec((None, rpad, d_attn), per_seq3),
                  pl.BlockSpec((None, nt, nh, dh), per_seq4),
                  pl.BlockSpec((None, nt, nh, dh), per_seq4),
                  pl.BlockSpec((nh * rpad, 1), lambda b, g, pt: (0, 0))]
                 + page_specs * 2,
        out_specs=pl.BlockSpec((None, rpad, d_attn), per_seq3),
        scratch_shapes=[pltpu.VMEM((nh * rpad, dh), F32), pltpu.VMEM((nh * rpad, 1), F32),
                        pltpu.VMEM((nh * rpad, dh), F32)],
    )
    return pl.pallas_call(
        functools.partial(_sb_sample_kernel, n_pg=n_pg, scale=dh ** -0.5),
        grid_spec=grid_spec,
        out_shape=jax.ShapeDtypeStruct((nb, rpad, d_attn), F32),
        compiler_params=_params(2),
        name="sb_attention_sample",
    )(page_table, q_pad, k_new, v_new, bias_col,
      *([cache_k] * len(page_specs)), *([cache_v] * len(page_specs)))


def kernel(x_prompt, x_sample, state_conv, state_pool, cache_k, cache_v, page_table, ln_g, ln_b, mix_w_in, conv_w, pool_w, pool_scale, mix_w_out, attn_w_qkv, attn_w_o, attn_bias, ffn_w_gate, ffn_w_up, ffn_w_down):
    bp, seq, d = x_prompt.shape
    bs, ts_, _ = x_sample.shape
    mp, ms = bp * seq, bs * ts_
    past_len = page_table.shape[1] * PAGE_SIZE
    xp = x_prompt.reshape(mp, d)
    xs = x_sample.reshape(ms, d)
    dp = pool_scale.shape[1]
    dc = conv_w.shape[2]
    d_attn = attn_w_o.shape[1]
    dh = d_attn // N_HEADS
    ln_g = ln_g.reshape(-1, 1, d)
    ln_b = ln_b.reshape(-1, 1, d)

    def ffn(xp, xpb, xs, xsb, layer):
        hp, hs, wd = ffn_up(xpb, xsb, ffn_w_gate, ffn_w_up, ffn_w_down, layer, 1024, 512, f"ffn_up_l{layer}")
        return matmul_residual_ln(hp, hs, wd, 0, xp, xs, ln_g, ln_b, 2 * layer + 1, 256,
                                  f"ffn_down_ln_l{layer}")

    wp = pool_w[0].astype(BF16)
    sp = pool_scale[0].reshape(1, dp)
    hp, hs = matmul(xp, xs, mix_w_in, 0, 0, mix_w_in.shape[2], F32, 512, 1024, "mix_in")
    zp, conv_p, pool_p = mixer_prompt(hp, conv_w[0], wp, sp, bp, seq, 256)
    zs, conv_s, pool_s = mixer_sample(hs, state_conv[0], state_pool[0], conv_w[0], wp, sp, bs, ts_, past_len)
    xp, xpb, xs, xsb = matmul_residual_ln(zp, zs, mix_w_out, 0, xp, xs, ln_g, ln_b, 0, 256, "mix_out_ln")
    xp, xpb, xs, xsb = ffn(xp, xpb, xs, xsb, 0)

    bias = attn_bias[0]
    (qp, qs), (kp, ks), (vp, vs) = (
        matmul(xpb, xsb, attn_w_qkv, 0, c * d_attn, d_attn, F32, 1024, 1024, "qkv") for c in range(3))
    op = sb_attention_prompt(qp, kp, vp, bias, bp, seq, 256)
    ks4 = ks.reshape(bs, ts_, N_HEADS, dh)
    vs4 = vs.reshape(bs, ts_, N_HEADS, dh)
    o_pad = sb_attention_sample(qs.reshape(bs, ts_, d_attn), ks4, vs4, bias, cache_k[0], cache_v[0],
                                page_table, 8)
    os_ = o_pad[:, :ts_, :].reshape(ms, d_attn)
    xp, xpb, xs, xsb = matmul_residual_ln(op, os_, attn_w_o, 0, xp, xs, ln_g, ln_b, 2, 256, "attn_out_ln")
    xp, _, xs, _ = ffn(xp, xpb, xs, xsb, 1)

    return (xp.reshape(bp, seq, d), xs.reshape(bs, ts_, d),
            conv_p[None], conv_s[None], pool_p[None], pool_s[None],
            kp.reshape(1, bp, seq, N_HEADS, dh), vp.reshape(1, bp, seq, N_HEADS, dh),
            ks4[None], vs4[None])
```

```python
import functools

import jax
import jax.numpy as jnp
from jax import lax
from jax.experimental import pallas as pl
from jax.experimental.pallas import tpu as pltpu

F32 = jnp.float32
BF16 = jnp.bfloat16

CONV_WIDTH = 3
POOL_WINDOWS = (2, 4, 8, 16)
POOL_BUF = max(POOL_WINDOWS) - 1
N_HEADS = 16
PAGE_SIZE = 128
DEPTH = 2
ALPHA = (2.0 * DEPTH) ** 0.25
LN_EPS = 1e-5

VMEM_LIMIT_BYTES = 60 * 1024 * 1024
SUBLANES = 8


def _params(n_axes):
    return pltpu.CompilerParams(
        dimension_semantics=("arbitrary",) * n_axes,
        vmem_limit_bytes=VMEM_LIMIT_BYTES,
    )


def _dot(x, w):
    return jnp.dot(x.astype(BF16), w, preferred_element_type=F32)


def _mm_kernel(xm_ref, xe_ref, w_ref, om_ref, oe_ref, wb_ref):
    @pl.when(pl.program_id(1) == 0)
    def _():
        wb_ref[...] = w_ref[...].astype(BF16)
        oe_ref[...] = _dot(xe_ref[...], wb_ref[...]).astype(oe_ref.dtype)

    om_ref[...] = _dot(xm_ref[...], wb_ref[...]).astype(om_ref.dtype)


def matmul(xm, xe, w, layer, col0, n, out_dtype, tm, tn, name):
    m, k = xm.shape
    me = xe.shape[0]
    cb0 = col0 // tn
    return pl.pallas_call(
        _mm_kernel,
        grid=(n // tn, m // tm),
        in_specs=[pl.BlockSpec((tm, k), lambda j, i: (i, 0)),
                  pl.BlockSpec((me, k), lambda j, i: (0, 0)),
                  pl.BlockSpec((None, k, tn), lambda j, i: (layer, 0, cb0 + j))],
        out_specs=[pl.BlockSpec((tm, tn), lambda j, i: (i, j)),
                   pl.BlockSpec((me, tn), lambda j, i: (0, j))],
        out_shape=[jax.ShapeDtypeStruct((m, n), out_dtype), jax.ShapeDtypeStruct((me, n), out_dtype)],
        scratch_shapes=[pltpu.VMEM((k, tn), BF16)],
        compiler_params=_params(2),
        name=name,
    )(xm, xe, w)


def _mm_heads_kernel(xm_ref, xe_ref, w_ref, *refs, dh, row_major):
    om_ref = refs[0] if row_major else None
    oh_ref, oe_ref, wb_ref = refs[-3:]

    @pl.when(pl.program_id(1) == 0)
    def _():
        wb_ref[...] = w_ref[...].astype(BF16)
        oe_ref[...] = _dot(xe_ref[...], wb_ref[...])

    res = _dot(xm_ref[...], wb_ref[...])
    if row_major:
        om_ref[...] = res
    for h in range(oh_ref.shape[0]):
        oh_ref[h] = res[:, h * dh:(h + 1) * dh].astype(BF16)


def matmul_heads(xm, xe, w, layer, col0, n, dh, tm, tn, name, row_major):
    m, k = xm.shape
    me = xe.shape[0]
    cb0 = col0 // tn
    hpt = tn // dh
    out_specs = [pl.BlockSpec((hpt, tm, dh), lambda j, i: (j, i, 0)),
                 pl.BlockSpec((me, tn), lambda j, i: (0, j))]
    out_shape = [jax.ShapeDtypeStruct((n // dh, m, dh), BF16), jax.ShapeDtypeStruct((me, n), F32)]
    if row_major:
        out_specs.insert(0, pl.BlockSpec((tm, tn), lambda j, i: (i, j)))
        out_shape.insert(0, jax.ShapeDtypeStruct((m, n), F32))
    return pl.pallas_call(
        functools.partial(_mm_heads_kernel, dh=dh, row_major=row_major),
        grid=(n // tn, m // tm),
        in_specs=[pl.BlockSpec((tm, k), lambda j, i: (i, 0)),
                  pl.BlockSpec((me, k), lambda j, i: (0, 0)),
                  pl.BlockSpec((None, k, tn), lambda j, i: (layer, 0, cb0 + j))],
        out_specs=out_specs,
        out_shape=out_shape,
        scratch_shapes=[pltpu.VMEM((k, tn), BF16)],
        compiler_params=_params(2),
        name=name,
    )(xm, xe, w)


def _layer_norm_rows(s, g, b):
    mu = jnp.mean(s, axis=-1, keepdims=True)
    d = s - mu
    var = jnp.mean(d * d, axis=-1, keepdims=True)
    return d * lax.rsqrt(var + LN_EPS) * g + b


def _mm_ln_kernel(xm_ref, xe_ref, w_ref, rm_ref, re_ref, g_ref, b_ref,
                  ym_ref, ymb_ref, ye_ref, yeb_ref, *wb_refs):
    wb_ref = wb_refs[0] if wb_refs else w_ref

    def rows(x_ref, r_ref, y_ref, yb_ref):
        if len(x_ref.shape) == 3:
            x = jnp.concatenate([x_ref[h] for h in range(x_ref.shape[0])], axis=1)
        else:
            x = x_ref[...]
        y = _layer_norm_rows(ALPHA * r_ref[...] + _dot(x, wb_ref[...]), g_ref[...], b_ref[...])
        y_ref[...] = y
        yb_ref[...] = y.astype(BF16)

    @pl.when(pl.program_id(0) == 0)
    def _():
        if wb_refs:
            wb_ref[...] = w_ref[...].astype(BF16)
        rows(xe_ref, re_ref, ye_ref, yeb_ref)

    rows(xm_ref, rm_ref, ym_ref, ymb_ref)


def matmul_residual_ln(xm, xe, w, layer, rm, re, ln_g, ln_b, ln_row, tm, name):
    me, k = xe.shape
    m = xm.shape[-2]
    d = w.shape[-1]
    row = lambda i: (i, 0)
    const = lambda i: (0, 0)
    if xm.ndim == 3:
        xm_spec = pl.BlockSpec((xm.shape[0], tm, xm.shape[2]), lambda i: (0, i, 0))
    else:
        xm_spec = pl.BlockSpec((tm, k), row)
    if w.ndim == 2:
        w_spec = pl.BlockSpec((k, d), const, pipeline_mode=pl.Buffered(1))
        scratch = []
    else:
        w_spec = pl.BlockSpec((None, k, d), lambda i: (layer, 0, 0), pipeline_mode=pl.Buffered(1))
        scratch = [pltpu.VMEM((k, d), BF16)]
    ln_spec = pl.BlockSpec((None, 1, d), lambda i: (ln_row, 0, 0))
    return pl.pallas_call(
        _mm_ln_kernel,
        grid=(m // tm,),
        in_specs=[xm_spec, pl.BlockSpec((me, k), const), w_spec,
                  pl.BlockSpec((tm, d), row), pl.BlockSpec((me, d), const), ln_spec, ln_spec],
        out_specs=[pl.BlockSpec((tm, d), row), pl.BlockSpec((tm, d), row),
                   pl.BlockSpec((me, d), const), pl.BlockSpec((me, d), const)],
        out_shape=[jax.ShapeDtypeStruct((m, d), F32), jax.ShapeDtypeStruct((m, d), BF16),
                   jax.ShapeDtypeStruct((me, d), F32), jax.ShapeDtypeStruct((me, d), BF16)],
        scratch_shapes=scratch,
        compiler_params=_params(1),
        name=name,
    )(xm, xe, w, rm, re, ln_g, ln_b)


def _ffn_up_kernel(xm_ref, xe_ref, wg_ref, wu_ref, wd_ref, hm_ref, he_ref, wdb_ref, wgb_ref, wub_ref):
    def rows(x_ref, h_ref):
        x = x_ref[...]
        h_ref[...] = (jax.nn.silu(_dot(x, wgb_ref[...])) * _dot(x, wub_ref[...])).astype(h_ref.dtype)

    @pl.when(pl.program_id(1) == 0)
    def _():
        wgb_ref[...] = wg_ref[...].astype(BF16)
        wub_ref[...] = wu_ref[...].astype(BF16)
        rows(xe_ref, he_ref)

    wdb_ref[...] = wd_ref[...].astype(BF16)
    rows(xm_ref, hm_ref)


def ffn_up(xm, xe, wg, wu, wd, layer, tm, tf, name):
    m, k = xm.shape
    me = xe.shape[0]
    f = wg.shape[2]
    d = wd.shape[2]
    n_row_steps = m // tm
    slab = tf // n_row_steps
    col = lambda j, i: (layer, 0, j)
    return pl.pallas_call(
        _ffn_up_kernel,
        grid=(f // tf, m // tm),
        in_specs=[pl.BlockSpec((tm, k), lambda j, i: (i, 0)),
                  pl.BlockSpec((me, k), lambda j, i: (0, 0)),
                  pl.BlockSpec((None, k, tf), col), pl.BlockSpec((None, k, tf), col),
                  pl.BlockSpec((None, slab, d), lambda j, i: (layer, j * n_row_steps + i, 0))],
        out_specs=[pl.BlockSpec((tm, tf), lambda j, i: (i, j)),
                   pl.BlockSpec((me, tf), lambda j, i: (0, j)),
                   pl.BlockSpec((slab, d), lambda j, i: (j * n_row_steps + i, 0))],
        out_shape=[jax.ShapeDtypeStruct((m, f), BF16), jax.ShapeDtypeStruct((me, f), BF16),
                   jax.ShapeDtypeStruct((f, d), BF16)],
        scratch_shapes=[pltpu.VMEM((k, tf), BF16), pltpu.VMEM((k, tf), BF16)],
        compiler_params=_params(2),
        name=name,
    )(xm, xe, wg, wu, wd)


def _mixer_prompt_kernel(h_ref, wc_ref, wp_ref, sp_ref, z_ref, co_ref, po_ref, cbuf, pbuf, *, ts, dc):
    i = pl.program_id(1)
    ch, ph = SUBLANES, 2 * SUBLANES
    dp = pbuf.shape[1]
    pg = dp // len(POOL_WINDOWS)

    @pl.when(i == 0)
    def _():
        cbuf[0:ch, :] = jnp.zeros((ch, dc), F32)
        pbuf[0:ph, :] = jnp.zeros((ph, dp), F32)

    gb = h_ref[:, 0:dc]
    u = h_ref[:, dc:2 * dc] * h_ref[:, 2 * dc:3 * dc]
    pv = h_ref[:, 3 * dc:3 * dc + dp]
    cbuf[ch:ch + ts, :] = u
    pbuf[ph:ph + ts, :] = pv

    y = cbuf[ch - 2:ch - 2 + ts, :] * wc_ref[0:1, :]
    y = y + cbuf[ch - 1:ch - 1 + ts, :] * wc_ref[1:2, :]
    y = y + u * wc_ref[2:3, :]
    z_ref[:, 0:dc] = (gb * y).astype(z_ref.dtype)

    pos = i * ts + lax.broadcasted_iota(jnp.int32, (ts, 1), 0)
    for gi, w in enumerate(POOL_WINDOWS):
        lo = gi * pg
        cur = pv[:, lo:lo + pg]
        s = cur
        for back in range(1, w):
            s = s + pbuf[ph - back:ph - back + ts, lo:lo + pg]
        cnt = jnp.minimum(w, pos + 1).astype(F32)
        dlt = s / cnt - cur
        yg = jnp.dot(dlt.astype(BF16), wp_ref[gi], preferred_element_type=F32)
        z_ref[:, dc + lo:dc + lo + pg] = (yg * sp_ref[:, lo:lo + pg]).astype(z_ref.dtype)

    co_ref[...] = cbuf[ch + ts - (CONV_WIDTH - 1):ch + ts, :]
    po_ref[...] = pbuf[ph + ts - POOL_BUF:ph + ts, :]
    cbuf[0:ch, :] = cbuf[ts:ts + ch, :]
    pbuf[0:ph, :] = pbuf[ts:ts + ph, :]


def mixer_prompt(h, wc, wp, sp, batch, seq, ts):
    rows, width = h.shape
    dp = sp.shape[1]
    dc = (width - dp) // 3
    nt = seq // ts
    return pl.pallas_call(
        functools.partial(_mixer_prompt_kernel, ts=ts, dc=dc),
        grid=(batch, nt),
        in_specs=[pl.BlockSpec((ts, width), lambda b, i: (b * nt + i, 0)),
                  pl.BlockSpec(wc.shape, lambda b, i: (0, 0)),
                  pl.BlockSpec(wp.shape, lambda b, i: (0, 0, 0)),
                  pl.BlockSpec(sp.shape, lambda b, i: (0, 0))],
        out_specs=[pl.BlockSpec((ts, dc + dp), lambda b, i: (b * nt + i, 0)),
                   pl.BlockSpec((None, CONV_WIDTH - 1, dc), lambda b, i: (b, 0, 0)),
                   pl.BlockSpec((None, POOL_BUF, dp), lambda b, i: (b, 0, 0))],
        out_shape=[jax.ShapeDtypeStruct((rows, dc + dp), BF16),
                   jax.ShapeDtypeStruct((batch, CONV_WIDTH - 1, dc), F32),
                   jax.ShapeDtypeStruct((batch, POOL_BUF, dp), F32)],
        scratch_shapes=[pltpu.VMEM((ts + SUBLANES, dc), F32), pltpu.VMEM((ts + 2 * SUBLANES, dp), F32)],
        compiler_params=_params(2),
        name="mixer_prompt",
    )(h, wc, wp, sp)


def _mixer_sample_kernel(h_ref, sc_ref, spool_ref, wc_ref, wp_ref, sp_ref, z_ref, co_ref, po_ref,
                         *, nb, nt, dc, pos0):
    dp = sp_ref.shape[1]
    pg = dp // len(POOL_WINDOWS)
    nc = CONV_WIDTH - 1

    conv_ext = [sc_ref[j] for j in range(nc)]
    pool_ext = [spool_ref[j] for j in range(POOL_BUF)]
    gbs = []
    for t in range(nt):
        gbs.append(h_ref[t, :, 0:dc])
        conv_ext.append(h_ref[t, :, dc:2 * dc] * h_ref[t, :, 2 * dc:3 * dc])
        pool_ext.append(h_ref[t, :, 3 * dc:3 * dc + dp])

    for t in range(nt):
        y = conv_ext[t] * wc_ref[0:1, :]
        for kk in range(1, CONV_WIDTH):
            y = y + conv_ext[t + kk] * wc_ref[kk:kk + 1, :]
        z_ref[t, :, 0:dc] = gbs[t] * y
    for j in range(nc):
        co_ref[j] = conv_ext[nt + j]

    for gi, w in enumerate(POOL_WINDOWS):
        lo = gi * pg
        dls = []
        for t in range(nt):
            e = POOL_BUF + t
            s = pool_ext[e][:, lo:lo + pg]
            for back in range(1, w):
                s = s + pool_ext[e - back][:, lo:lo + pg]
            cnt = float(min(w, pos0 + t + 1))
            dls.append(s / cnt - pool_ext[e][:, lo:lo + pg])
        dl = jnp.concatenate(dls, axis=0).astype(BF16)
        yg = jnp.dot(dl, wp_ref[gi], preferred_element_type=F32) * sp_ref[:, lo:lo + pg]
        for t in range(nt):
            z_ref[t, :, dc + lo:dc + lo + pg] = yg[t * nb:(t + 1) * nb, :]
    for j in range(POOL_BUF):
        po_ref[j] = pool_ext[nt + j]


def mixer_sample(h, state_conv, state_pool, wc, wp, sp, nb, nt, pos0):
    rows, width = h.shape
    dp = sp.shape[1]
    dc = (width - dp) // 3
    nc = CONV_WIDTH - 1
    step_major = lambda a: jnp.swapaxes(a, 0, 1)
    z, conv_new, pool_new = pl.pallas_call(
        functools.partial(_mixer_sample_kernel, nb=nb, nt=nt, dc=dc, pos0=pos0),
        out_shape=[jax.ShapeDtypeStruct((nt, nb, dc + dp), F32),
                   jax.ShapeDtypeStruct((nc, nb, dc), F32),
                   jax.ShapeDtypeStruct((POOL_BUF, nb, dp), F32)],
        compiler_params=pltpu.CompilerParams(vmem_limit_bytes=VMEM_LIMIT_BYTES),
        name="mixer_sample",
    )(step_major(h.reshape(nb, nt, width)), step_major(state_conv), step_major(state_pool), wc, wp, sp)
    return step_major(z).reshape(rows, dc + dp), step_major(conv_new), step_major(pool_new)


def _neg_softplus(z):
    return -(jnp.maximum(z, 0.0) + jnp.log1p(jnp.exp(-jnp.abs(z))))


def _suffix_sum(ls, tri):
    hi = ls.astype(BF16)
    lo = (ls - hi.astype(F32)).astype(BF16)
    return (jnp.dot(hi, tri, preferred_element_type=F32)
            + jnp.dot(lo, tri, preferred_element_type=F32))


def _lower_tri(n):
    r = lax.broadcasted_iota(jnp.int32, (n, n), 0)
    c = lax.broadcasted_iota(jnp.int32, (n, n), 1)
    return jnp.where(r >= c, 1.0, 0.0).astype(BF16)


def _sb_prompt_kernel(ij_ref, bias_ref, qb_ref, kb_ref, vb_ref, o_ref,
                      zr_ref, z_ref, hi_ref, lo_ref, e_ref, c_ref, acc_ref, *, tb, scale):
    npairs = ij_ref.shape[1]
    bias = bias_ref[pl.program_id(1)]
    for ref in (zr_ref, z_ref, hi_ref, lo_ref, e_ref, c_ref, acc_ref):
        ref[...] = jnp.zeros(ref.shape, ref.dtype)
    tri = _lower_tri(tb)
    key_minus_query = (lax.broadcasted_iota(jnp.int32, (tb, tb), 1)
                       - lax.broadcasted_iota(jnp.int32, (tb, tb), 0))

    def pair(p):
        p = jnp.clip(p, 0, npairs - 1)
        return ij_ref[0, p], ij_ref[1, p]

    def block_rows(ref, blk):
        return ref[pl.ds(pl.multiple_of(blk * tb, tb), tb), :]

    def visible(i, j):
        return key_minus_query < jnp.where(i == j, 0, tb)

    def step(n, s):
        t = 1 - s
        i, j = pair(n)
        zr_ref[s] = lax.dot_general(block_rows(qb_ref, i), block_rows(kb_ref, j),
                                    (((1,), (1,)), ((), ())), preferred_element_type=F32)

        i, j = pair(n - 1)
        z = zr_ref[t] * scale + bias
        ls = -(jnp.maximum(z, 0.0) + jnp.log(1.0 + jnp.exp(-jnp.abs(z))))
        ls = jnp.where(visible(i, j), ls, 0.0)
        hi = ls.astype(BF16)
        z_ref[t] = z
        hi_ref[t] = hi
        lo_ref[t] = (ls - hi.astype(F32)).astype(BF16)

        i, j = pair(n - 2)
        rc = (jnp.dot(hi_ref[s], tri, preferred_element_type=F32)
              + jnp.dot(lo_ref[s], tri, preferred_element_type=F32))
        c = jnp.where(i == j, 0.0, c_ref[...])
        e = jnp.exp(z_ref[s] + rc + c)
        e_ref[s] = jnp.where(visible(i, j), e, 0.0).astype(BF16)
        c_ref[...] = c + rc[:, 0:1]

        i, j = pair(n - 3)
        acc = jnp.where(i == j, 0.0, acc_ref[...])
        new = acc + jnp.dot(e_ref[t], block_rows(vb_ref, j), preferred_element_type=F32)
        acc = jnp.where(n - 3 < npairs, new, acc)
        acc_ref[...] = acc
        o_ref[pl.ds(pl.multiple_of(i * tb, tb), tb), :] = acc.astype(o_ref.dtype)

    def two_steps(m, carry):
        step(2 * m, 0)
        step(2 * m + 1, 1)
        return carry

    n_steps = npairs + 3
    lax.fori_loop(0, (n_steps + 1) // 2, two_steps, 0)


def sb_attention_prompt(q, k, v, bias, batch, seq, tb):
    nh, rows, dh = q.shape
    nblk = seq // tb
    pairs = [(i, j) for i in range(nblk) for j in range(i, -1, -1)]
    ij = jnp.asarray(pairs, dtype=jnp.int32).T
    blk = pl.BlockSpec((None, seq, dh), lambda b, h: (h, b, 0))
    smem = pl.BlockSpec(memory_space=pltpu.SMEM)
    stage_f32 = pltpu.VMEM((2, tb, tb), F32)
    stage_bf16 = pltpu.VMEM((2, tb, tb), BF16)
    return pl.pallas_call(
        functools.partial(_sb_prompt_kernel, tb=tb, scale=dh ** -0.5),
        grid=(batch, nh),
        in_specs=[smem, smem, blk, blk, blk],
        out_specs=blk,
        out_shape=jax.ShapeDtypeStruct((nh, rows, dh), BF16),
        scratch_shapes=[stage_f32, stage_f32, stage_bf16, stage_bf16, stage_bf16,
                          pltpu.VMEM((tb, 1), F32), pltpu.VMEM((tb, dh), F32)],
        compiler_params=_params(2),
        name="sb_attention_prompt",
    )(ij, bias, q, k, v)


def _sb_sample_kernel(pt_ref, q_ref, kn_ref, vn_ref, bias_ref, *refs, n_pg, scale):
    del pt_ref
    nt, nh, dh = kn_ref.shape
    ngrp = nh // SUBLANES
    k_refs = refs[:n_pg * ngrp]
    v_refs = refs[n_pg * ngrp:2 * n_pg * ngrp]
    o_ref, qf_ref, c_ref, acc_ref = refs[2 * n_pg * ngrp:]

    def head_rows(page_refs, p, h):
        flat = page_refs[p * ngrp + h // SUBLANES].reshape(PAGE_SIZE * SUBLANES, dh)
        return flat[pl.ds(h % SUBLANES, PAGE_SIZE, stride=SUBLANES), :]

    rpad = qf_ref.shape[0] // nh
    g = pl.program_id(1)
    bias_col = bias_ref[...]

    @pl.when(g == 0)
    def _():
        for h in range(nh):
            qf_ref[h * rpad:(h + 1) * rpad, :] = q_ref[:, h * dh:(h + 1) * dh]
        rr = lax.broadcasted_iota(jnp.int32, (nh * rpad, nh), 0)
        hh = lax.broadcasted_iota(jnp.int32, (nh * rpad, nh), 1)
        rep = jnp.where(rr // rpad == hh, 1.0, 0.0).astype(BF16)
        step = lax.broadcasted_iota(jnp.int32, (nh * rpad, 1), 0) % rpad
        qr = qf_ref[...].astype(BF16).astype(F32)
        zs, lss, valids = [], [], []
        for s in range(nt):
            kr = jnp.dot(rep, kn_ref[s].astype(BF16), preferred_element_type=F32)
            z = jnp.sum(qr * kr, axis=1, keepdims=True) * scale + bias_col
            valid = step > s
            zs.append(z)
            valids.append(valid)
            lss.append(jnp.where(valid, _neg_softplus(z), 0.0))
        acc = jnp.zeros(acc_ref.shape, F32)
        rc = jnp.zeros((nh * rpad, 1), F32)
        for s in reversed(range(nt)):
            rc = rc + lss[s]
            a = jnp.where(valids[s], jnp.exp(zs[s] + rc), 0.0)
            vr = jnp.dot(rep, vn_ref[s].astype(BF16), preferred_element_type=F32)
            acc = acc + a.astype(BF16).astype(F32) * vr
        acc_ref[...] = acc
        c_ref[...] = rc

    bf16_values = lambda x: x.astype(BF16).astype(F32)
    width = n_pg * PAGE_SIZE
    zrows = []
    for h in range(nh):
        qh = bf16_values(qf_ref[h * rpad:(h + 1) * rpad, :])
        kh = jnp.concatenate([head_rows(k_refs, p, h) for p in range(n_pg)], axis=0)
        zrows.append(lax.dot_general(qh, kh, (((1,), (1,)), ((), ())), preferred_element_type=F32))
    z = jnp.concatenate(zrows, axis=0) * scale + bias_col
    ls = _neg_softplus(z)
    tb = 2 * PAGE_SIZE
    tri = _lower_tri(tb)
    c = c_ref[...]
    a_blocks = [None] * (width // tb)
    for blk in reversed(range(width // tb)):
        rc = _suffix_sum(ls[:, blk * tb:(blk + 1) * tb], tri)
        a_blocks[blk] = bf16_values(jnp.exp(z[:, blk * tb:(blk + 1) * tb] + rc + c))
        c = c + rc[:, 0:1]
    c_ref[...] = c
    a = jnp.concatenate(a_blocks, axis=1)
    orows = []
    for h in range(nh):
        vh = jnp.concatenate([head_rows(v_refs, p, h) for p in range(n_pg)], axis=0)
        orows.append(jnp.dot(a[h * rpad:(h + 1) * rpad, :], vh, preferred_element_type=F32))
    acc_ref[...] += jnp.concatenate(orows, axis=0)

    @pl.when(g == pl.num_programs(1) - 1)
    def _():
        for h in range(nh):
            o_ref[:, h * dh:(h + 1) * dh] = acc_ref[h * rpad:(h + 1) * rpad, :]


def sb_attention_sample(q, k_new, v_new, bias, cache_k, cache_v, page_table, n_pg):
    nb, nt, d_attn = q.shape
    nh, dh = k_new.shape[2], k_new.shape[3]
    n_pages = page_table.shape[1]
    rpad = SUBLANES
    q_pad = jnp.pad(q, ((0, 0), (0, rpad - nt), (0, 0)))
    bias_col = jnp.repeat(bias.astype(F32), rpad).reshape(nh * rpad, 1)
    n_steps = n_pages // n_pg

    ngrp = nh // SUBLANES
    cache_k = cache_k.reshape(-1, PAGE_SIZE, ngrp, SUBLANES, dh)
    cache_v = cache_v.reshape(-1, PAGE_SIZE, ngrp, SUBLANES, dh)

    def page_spec(p, grp):
        return pl.BlockSpec(
            (None, PAGE_SIZE, None, SUBLANES, dh),
            lambda b, g, pt: (pt[b, n_pages - (g + 1) * n_pg + p], 0, grp, 0, 0))

    page_specs = [page_spec(p, grp) for p in range(n_pg) for grp in range(ngrp)]

    per_seq3 = lambda b, g, pt: (b, 0, 0)
    per_seq4 = lambda b, g, pt: (b, 0, 0, 0)
    grid_spec = pltpu.PrefetchScalarGridSpec(
        num_scalar_prefetch=1,
        grid=(nb, n_steps),
        in_specs=[pl.BlockSpec((None, rpad, d_attn), per_seq3),
                  pl.BlockSpec((None, nt, nh, dh), per_seq4),
                  pl.BlockSpec((None, nt, nh, dh), per_seq4),
                  pl.BlockSpec((nh * rpad, 1), lambda b, g, pt: (0, 0))]
                 + page_specs * 2,
        out_specs=pl.BlockSpec((None, rpad, d_attn), per_seq3),
        scratch_shapes=[pltpu.VMEM((nh * rpad, dh), F32), pltpu.VMEM((nh * rpad, 1), F32),
                        pltpu.VMEM((nh * rpad, dh), F32)],
    )
    return pl.pallas_call(
        functools.partial(_sb_sample_kernel, n_pg=n_pg, scale=dh ** -0.5),
        grid_spec=grid_spec,
        out_shape=jax.ShapeDtypeStruct((nb, rpad, d_attn), F32),
        compiler_params=_params(2),
        name="sb_attention_sample",
    )(page_table, q_pad, k_new, v_new, bias_col,
      *([cache_k] * len(page_specs)), *([cache_v] * len(page_specs)))


def kernel(x_prompt, x_sample, state_conv, state_pool, cache_k, cache_v, page_table, ln_g, ln_b, mix_w_in, conv_w, pool_w, pool_scale, mix_w_out, attn_w_qkv, attn_w_o, attn_bias, ffn_w_gate, ffn_w_up, ffn_w_down):
    bp, seq, d = x_prompt.shape
    bs, ts_, _ = x_sample.shape
    mp, ms = bp * seq, bs * ts_
    past_len = page_table.shape[1] * PAGE_SIZE
    xp = x_prompt.reshape(mp, d)
    xs = x_sample.reshape(ms, d)
    dp = pool_scale.shape[1]
    dc = conv_w.shape[2]
    d_attn = attn_w_o.shape[1]
    dh = d_attn // N_HEADS
    ln_g = ln_g.reshape(-1, 1, d)
    ln_b = ln_b.reshape(-1, 1, d)

    def ffn(xp, xpb, xs, xsb, layer):
        hp, hs, wd = ffn_up(xpb, xsb, ffn_w_gate, ffn_w_up, ffn_w_down, layer, 1024, 512, f"ffn_up_l{layer}")
        return matmul_residual_ln(hp, hs, wd, 0, xp, xs, ln_g, ln_b, 2 * layer + 1, 256,
                                  f"ffn_down_ln_l{layer}")

    wp = pool_w[0].astype(BF16)
    sp = pool_scale[0].reshape(1, dp)
    hp, hs = matmul(xp, xs, mix_w_in, 0, 0, mix_w_in.shape[2], F32, 512, 1024, "mix_in")
    zp, conv_p, pool_p = mixer_prompt(hp, conv_w[0], wp, sp, bp, seq, 256)
    zs, conv_s, pool_s = mixer_sample(hs, state_conv[0], state_pool[0], conv_w[0], wp, sp, bs, ts_, past_len)
    xp, xpb, xs, xsb = matmul_residual_ln(zp, zs, mix_w_out, 0, xp, xs, ln_g, ln_b, 0, 256, "mix_out_ln")
    xp, xpb, xs, xsb = ffn(xp, xpb, xs, xsb, 0)

    bias = attn_bias[0]
    qh, qs = matmul_heads(xpb, xsb, attn_w_qkv, 0, 0, d_attn, dh, 1024, 1024, "q_proj", False)
    kp, kh, ks = matmul_heads(xpb, xsb, attn_w_qkv, 0, d_attn, d_attn, dh, 1024, 1024, "k_proj", True)
    vp, vh, vs = matmul_heads(xpb, xsb, attn_w_qkv, 0, 2 * d_attn, d_attn, dh, 1024, 1024, "v_proj", True)
    op = sb_attention_prompt(qh, kh, vh, bias, bp, seq, 256)
    ks4 = ks.reshape(bs, ts_, N_HEADS, dh)
    vs4 = vs.reshape(bs, ts_, N_HEADS, dh)
    o_pad = sb_attention_sample(qs.reshape(bs, ts_, d_attn), ks4, vs4, bias, cache_k[0], cache_v[0],
                                page_table, 8)
    os_ = o_pad[:, :ts_, :].reshape(ms, d_attn)
    xp, xpb, xs, xsb = matmul_residual_ln(op, os_, attn_w_o, 0, xp, xs, ln_g, ln_b, 2, 256, "attn_out_ln")
    xp, _, xs, _ = ffn(xp, xpb, xs, xsb, 1)

    return (xp.reshape(bp, seq, d), xs.reshape(bs, ts_, d),
            conv_p[None], conv_s[None], pool_p[None], pool_s[None],
            kp.reshape(1, bp, seq, N_HEADS, dh), vp.reshape(1, bp, seq, N_HEADS, dh),
            ks4[None], vs4[None])
```

```python
import functools

import jax
import jax.numpy as jnp
from jax import lax
from jax.experimental import pallas as pl
from jax.experimental.pallas import tpu as pltpu

F32 = jnp.float32
BF16 = jnp.bfloat16

CONV_WIDTH = 3
POOL_WINDOWS = (2, 4, 8, 16)
POOL_BUF = max(POOL_WINDOWS) - 1
N_HEADS = 16
PAGE_SIZE = 128
DEPTH = 2
ALPHA = (2.0 * DEPTH) ** 0.25
LN_EPS = 1e-5
MASKED_LOGIT = -1e30
STEPS_PER_TRIP = 8

VMEM_LIMIT_BYTES = 60 * 1024 * 1024
SUBLANES = 8


def _params(n_axes):
    return pltpu.CompilerParams(
        dimension_semantics=("arbitrary",) * n_axes,
        vmem_limit_bytes=VMEM_LIMIT_BYTES,
    )


def _dot(x, w):
    return jnp.dot(x.astype(BF16), w, preferred_element_type=F32)


def _mm_kernel(xm_ref, xe_ref, w_ref, om_ref, oe_ref, wb_ref):
    @pl.when(pl.program_id(1) == 0)
    def _():
        wb_ref[...] = w_ref[...].astype(BF16)
        oe_ref[...] = _dot(xe_ref[...], wb_ref[...]).astype(oe_ref.dtype)

    om_ref[...] = _dot(xm_ref[...], wb_ref[...]).astype(om_ref.dtype)


def matmul(xm, xe, w, layer, col0, n, out_dtype, tm, tn, name):
    m, k = xm.shape
    me = xe.shape[0]
    cb0 = col0 // tn
    return pl.pallas_call(
        _mm_kernel,
        grid=(n // tn, m // tm),
        in_specs=[pl.BlockSpec((tm, k), lambda j, i: (i, 0)),
                  pl.BlockSpec((me, k), lambda j, i: (0, 0)),
                  pl.BlockSpec((None, k, tn), lambda j, i: (layer, 0, cb0 + j))],
        out_specs=[pl.BlockSpec((tm, tn), lambda j, i: (i, j)),
                   pl.BlockSpec((me, tn), lambda j, i: (0, j))],
        out_shape=[jax.ShapeDtypeStruct((m, n), out_dtype), jax.ShapeDtypeStruct((me, n), out_dtype)],
        scratch_shapes=[pltpu.VMEM((k, tn), BF16)],
        compiler_params=_params(2),
        name=name,
    )(xm, xe, w)


def _mm_heads_kernel(xm_ref, xe_ref, w_ref, *refs, dh, row_major):
    om_ref = refs[0] if row_major else None
    oh_ref, oe_ref, wb_ref = refs[-3:]

    @pl.when(pl.program_id(1) == 0)
    def _():
        wb_ref[...] = w_ref[...].astype(BF16)
        oe_ref[...] = _dot(xe_ref[...], wb_ref[...])

    res = _dot(xm_ref[...], wb_ref[...])
    if row_major:
        om_ref[...] = res
    for h in range(oh_ref.shape[0]):
        oh_ref[h] = res[:, h * dh:(h + 1) * dh].astype(BF16)


def matmul_heads(xm, xe, w, layer, col0, n, dh, tm, tn, name, row_major):
    m, k = xm.shape
    me = xe.shape[0]
    cb0 = col0 // tn
    hpt = tn // dh
    out_specs = [pl.BlockSpec((hpt, tm, dh), lambda j, i: (j, i, 0)),
                 pl.BlockSpec((me, tn), lambda j, i: (0, j))]
    out_shape = [jax.ShapeDtypeStruct((n // dh, m, dh), BF16), jax.ShapeDtypeStruct((me, n), F32)]
    if row_major:
        out_specs.insert(0, pl.BlockSpec((tm, tn), lambda j, i: (i, j)))
        out_shape.insert(0, jax.ShapeDtypeStruct((m, n), F32))
    return pl.pallas_call(
        functools.partial(_mm_heads_kernel, dh=dh, row_major=row_major),
        grid=(n // tn, m // tm),
        in_specs=[pl.BlockSpec((tm, k), lambda j, i: (i, 0)),
                  pl.BlockSpec((me, k), lambda j, i: (0, 0)),
                  pl.BlockSpec((None, k, tn), lambda j, i: (layer, 0, cb0 + j))],
        out_specs=out_specs,
        out_shape=out_shape,
        scratch_shapes=[pltpu.VMEM((k, tn), BF16)],
        compiler_params=_params(2),
        name=name,
    )(xm, xe, w)


def _layer_norm_rows(s, g, b):
    mu = jnp.mean(s, axis=-1, keepdims=True)
    d = s - mu
    var = jnp.mean(d * d, axis=-1, keepdims=True)
    return d * lax.rsqrt(var + LN_EPS) * g + b


def _mm_ln_kernel(xm_ref, xe_ref, w_ref, rm_ref, re_ref, g_ref, b_ref,
                  ym_ref, ymb_ref, ye_ref, yeb_ref, *wb_refs):
    wb_ref = wb_refs[0] if wb_refs else w_ref

    def rows(x_ref, r_ref, y_ref, yb_ref):
        if len(x_ref.shape) == 3:
            x = jnp.concatenate([x_ref[h] for h in range(x_ref.shape[0])], axis=1)
        else:
            x = x_ref[...]
        y = _layer_norm_rows(ALPHA * r_ref[...] + _dot(x, wb_ref[...]), g_ref[...], b_ref[...])
        y_ref[...] = y
        yb_ref[...] = y.astype(BF16)

    @pl.when(pl.program_id(0) == 0)
    def _():
        if wb_refs:
            wb_ref[...] = w_ref[...].astype(BF16)
        rows(xe_ref, re_ref, ye_ref, yeb_ref)

    rows(xm_ref, rm_ref, ym_ref, ymb_ref)


def matmul_residual_ln(xm, xe, w, layer, rm, re, ln_g, ln_b, ln_row, tm, name):
    me, k = xe.shape
    m = xm.shape[-2]
    d = w.shape[-1]
    row = lambda i: (i, 0)
    const = lambda i: (0, 0)
    if xm.ndim == 3:
        xm_spec = pl.BlockSpec((xm.shape[0], tm, xm.shape[2]), lambda i: (0, i, 0))
    else:
        xm_spec = pl.BlockSpec((tm, k), row)
    if w.ndim == 2:
        w_spec = pl.BlockSpec((k, d), const, pipeline_mode=pl.Buffered(1))
        scratch = []
    else:
        w_spec = pl.BlockSpec((None, k, d), lambda i: (layer, 0, 0), pipeline_mode=pl.Buffered(1))
        scratch = [pltpu.VMEM((k, d), BF16)]
    ln_spec = pl.BlockSpec((None, 1, d), lambda i: (ln_row, 0, 0))
    return pl.pallas_call(
        _mm_ln_kernel,
        grid=(m // tm,),
        in_specs=[xm_spec, pl.BlockSpec((me, k), const), w_spec,
                  pl.BlockSpec((tm, d), row), pl.BlockSpec((me, d), const), ln_spec, ln_spec],
        out_specs=[pl.BlockSpec((tm, d), row), pl.BlockSpec((tm, d), row),
                   pl.BlockSpec((me, d), const), pl.BlockSpec((me, d), const)],
        out_shape=[jax.ShapeDtypeStruct((m, d), F32), jax.ShapeDtypeStruct((m, d), BF16),
                   jax.ShapeDtypeStruct((me, d), F32), jax.ShapeDtypeStruct((me, d), BF16)],
        scratch_shapes=scratch,
        compiler_params=_params(1),
        name=name,
    )(xm, xe, w, rm, re, ln_g, ln_b)


def _ffn_up_kernel(xm_ref, xe_ref, wg_ref, wu_ref, wd_ref, hm_ref, he_ref, wdb_ref, wgb_ref, wub_ref):
    def rows(x_ref, h_ref):
        x = x_ref[...]
        h_ref[...] = (jax.nn.silu(_dot(x, wgb_ref[...])) * _dot(x, wub_ref[...])).astype(h_ref.dtype)

    @pl.when(pl.program_id(1) == 0)
    def _():
        wgb_ref[...] = wg_ref[...].astype(BF16)
        wub_ref[...] = wu_ref[...].astype(BF16)
        rows(xe_ref, he_ref)

    wdb_ref[...] = wd_ref[...].astype(BF16)
    rows(xm_ref, hm_ref)


def ffn_up(xm, xe, wg, wu, wd, layer, tm, tf, name):
    m, k = xm.shape
    me = xe.shape[0]
    f = wg.shape[2]
    d = wd.shape[2]
    n_row_steps = m // tm
    slab = tf // n_row_steps
    col = lambda j, i: (layer, 0, j)
    return pl.pallas_call(
        _ffn_up_kernel,
        grid=(f // tf, m // tm),
        in_specs=[pl.BlockSpec((tm, k), lambda j, i: (i, 0)),
                  pl.BlockSpec((me, k), lambda j, i: (0, 0)),
                  pl.BlockSpec((None, k, tf), col), pl.BlockSpec((None, k, tf), col),
                  pl.BlockSpec((None, slab, d), lambda j, i: (layer, j * n_row_steps + i, 0))],
        out_specs=[pl.BlockSpec((tm, tf), lambda j, i: (i, j)),
                   pl.BlockSpec((me, tf), lambda j, i: (0, j)),
                   pl.BlockSpec((slab, d), lambda j, i: (j * n_row_steps + i, 0))],
        out_shape=[jax.ShapeDtypeStruct((m, f), BF16), jax.ShapeDtypeStruct((me, f), BF16),
                   jax.ShapeDtypeStruct((f, d), BF16)],
        scratch_shapes=[pltpu.VMEM((k, tf), BF16), pltpu.VMEM((k, tf), BF16)],
        compiler_params=_params(2),
        name=name,
    )(xm, xe, wg, wu, wd)


def _mixer_prompt_kernel(h_ref, wc_ref, wp_ref, sp_ref, z_ref, co_ref, po_ref, cbuf, pbuf, *, ts, dc):
    i = pl.program_id(1)
    ch, ph = SUBLANES, 2 * SUBLANES
    dp = pbuf.shape[1]
    pg = dp // len(POOL_WINDOWS)

    @pl.when(i == 0)
    def _():
        cbuf[0:ch, :] = jnp.zeros((ch, dc), F32)
        pbuf[0:ph, :] = jnp.zeros((ph, dp), F32)

    gb = h_ref[:, 0:dc]
    u = h_ref[:, dc:2 * dc] * h_ref[:, 2 * dc:3 * dc]
    pv = h_ref[:, 3 * dc:3 * dc + dp]
    cbuf[ch:ch + ts, :] = u
    pbuf[ph:ph + ts, :] = pv

    y = cbuf[ch - 2:ch - 2 + ts, :] * wc_ref[0:1, :]
    y = y + cbuf[ch - 1:ch - 1 + ts, :] * wc_ref[1:2, :]
    y = y + u * wc_ref[2:3, :]
    z_ref[:, 0:dc] = (gb * y).astype(z_ref.dtype)

    pos = i * ts + lax.broadcasted_iota(jnp.int32, (ts, 1), 0)
    for gi, w in enumerate(POOL_WINDOWS):
        lo = gi * pg
        cur = pv[:, lo:lo + pg]
        s = cur
        for back in range(1, w):
            s = s + pbuf[ph - back:ph - back + ts, lo:lo + pg]
        cnt = jnp.minimum(w, pos + 1).astype(F32)
        dlt = s / cnt - cur
        yg = jnp.dot(dlt.astype(BF16), wp_ref[gi], preferred_element_type=F32)
        z_ref[:, dc + lo:dc + lo + pg] = (yg * sp_ref[:, lo:lo + pg]).astype(z_ref.dtype)

    co_ref[...] = cbuf[ch + ts - (CONV_WIDTH - 1):ch + ts, :]
    po_ref[...] = pbuf[ph + ts - POOL_BUF:ph + ts, :]
    cbuf[0:ch, :] = cbuf[ts:ts + ch, :]
    pbuf[0:ph, :] = pbuf[ts:ts + ph, :]


def mixer_prompt(h, wc, wp, sp, batch, seq, ts):
    rows, width = h.shape
    dp = sp.shape[1]
    dc = (width - dp) // 3
    nt = seq // ts
    return pl.pallas_call(
        functools.partial(_mixer_prompt_kernel, ts=ts, dc=dc),
        grid=(batch, nt),
        in_specs=[pl.BlockSpec((ts, width), lambda b, i: (b * nt + i, 0)),
                  pl.BlockSpec(wc.shape, lambda b, i: (0, 0)),
                  pl.BlockSpec(wp.shape, lambda b, i: (0, 0, 0)),
                  pl.BlockSpec(sp.shape, lambda b, i: (0, 0))],
        out_specs=[pl.BlockSpec((ts, dc + dp), lambda b, i: (b * nt + i, 0)),
                   pl.BlockSpec((None, CONV_WIDTH - 1, dc), lambda b, i: (b, 0, 0)),
                   pl.BlockSpec((None, POOL_BUF, dp), lambda b, i: (b, 0, 0))],
        out_shape=[jax.ShapeDtypeStruct((rows, dc + dp), BF16),
                   jax.ShapeDtypeStruct((batch, CONV_WIDTH - 1, dc), F32),
                   jax.ShapeDtypeStruct((batch, POOL_BUF, dp), F32)],
        scratch_shapes=[pltpu.VMEM((ts + SUBLANES, dc), F32), pltpu.VMEM((ts + 2 * SUBLANES, dp), F32)],
        compiler_params=_params(2),
        name="mixer_prompt",
    )(h, wc, wp, sp)


def _mixer_sample_kernel(h_ref, sc_ref, spool_ref, wc_ref, wp_ref, sp_ref, z_ref, co_ref, po_ref,
                         *, nb, nt, dc, pos0):
    dp = sp_ref.shape[1]
    pg = dp // len(POOL_WINDOWS)
    nc = CONV_WIDTH - 1

    conv_ext = [sc_ref[j] for j in range(nc)]
    pool_ext = [spool_ref[j] for j in range(POOL_BUF)]
    gbs = []
    for t in range(nt):
        gbs.append(h_ref[t, :, 0:dc])
        conv_ext.append(h_ref[t, :, dc:2 * dc] * h_ref[t, :, 2 * dc:3 * dc])
        pool_ext.append(h_ref[t, :, 3 * dc:3 * dc + dp])

    for t in range(nt):
        y = conv_ext[t] * wc_ref[0:1, :]
        for kk in range(1, CONV_WIDTH):
            y = y + conv_ext[t + kk] * wc_ref[kk:kk + 1, :]
        z_ref[t, :, 0:dc] = gbs[t] * y
    for j in range(nc):
        co_ref[j] = conv_ext[nt + j]

    for gi, w in enumerate(POOL_WINDOWS):
        lo = gi * pg
        dls = []
        for t in range(nt):
            e = POOL_BUF + t
            s = pool_ext[e][:, lo:lo + pg]
            for back in range(1, w):
                s = s + pool_ext[e - back][:, lo:lo + pg]
            cnt = float(min(w, pos0 + t + 1))
            dls.append(s / cnt - pool_ext[e][:, lo:lo + pg])
        dl = jnp.concatenate(dls, axis=0).astype(BF16)
        yg = jnp.dot(dl, wp_ref[gi], preferred_element_type=F32) * sp_ref[:, lo:lo + pg]
        for t in range(nt):
            z_ref[t, :, dc + lo:dc + lo + pg] = yg[t * nb:(t + 1) * nb, :]
    for j in range(POOL_BUF):
        po_ref[j] = pool_ext[nt + j]


def mixer_sample(h, state_conv, state_pool, wc, wp, sp, nb, nt, pos0):
    rows, width = h.shape
    dp = sp.shape[1]
    dc = (width - dp) // 3
    nc = CONV_WIDTH - 1
    step_major = lambda a: jnp.swapaxes(a, 0, 1)
    z, conv_new, pool_new = pl.pallas_call(
        functools.partial(_mixer_sample_kernel, nb=nb, nt=nt, dc=dc, pos0=pos0),
        out_shape=[jax.ShapeDtypeStruct((nt, nb, dc + dp), F32),
                   jax.ShapeDtypeStruct((nc, nb, dc), F32),
                   jax.ShapeDtypeStruct((POOL_BUF, nb, dp), F32)],
        compiler_params=pltpu.CompilerParams(vmem_limit_bytes=VMEM_LIMIT_BYTES),
        name="mixer_sample",
    )(step_major(h.reshape(nb, nt, width)), step_major(state_conv), step_major(state_pool), wc, wp, sp)
    return step_major(z).reshape(rows, dc + dp), step_major(conv_new), step_major(pool_new)


def _neg_softplus(z):
    return -(jnp.maximum(z, 0.0) + jnp.log1p(jnp.exp(-jnp.abs(z))))


def _suffix_sum(ls, tri):
    hi = ls.astype(BF16)
    lo = (ls - hi.astype(F32)).astype(BF16)
    return (jnp.dot(hi, tri, preferred_element_type=F32)
            + jnp.dot(lo, tri, preferred_element_type=F32))


def _lower_tri(n):
    r = lax.broadcasted_iota(jnp.int32, (n, n), 0)
    c = lax.broadcasted_iota(jnp.int32, (n, n), 1)
    return jnp.where(r >= c, 1.0, 0.0).astype(BF16)


def _sb_prompt_kernel(ij_ref, bias_ref, qb_ref, kb_ref, vb_ref, o_ref,
                      zr_ref, z_ref, hi_ref, lo_ref, e_ref, c_ref, acc_ref, offset_ref, *, tb, scale):
    npairs = ij_ref.shape[1]
    bias = bias_ref[pl.program_id(1)]
    for ref in (zr_ref, z_ref, hi_ref, lo_ref, e_ref, c_ref, acc_ref):
        ref[...] = jnp.zeros(ref.shape, ref.dtype)
    tri = _lower_tri(tb)
    hidden = jnp.where(lax.broadcasted_iota(jnp.int32, (tb, tb), 1)
                       < lax.broadcasted_iota(jnp.int32, (tb, tb), 0), 0.0, MASKED_LOGIT)
    offset_ref[0] = jnp.full((tb, tb), bias, F32)
    offset_ref[1] = bias + hidden

    def pair(p):
        p = jnp.clip(p, 0, npairs - 1)
        return ij_ref[0, p], ij_ref[1, p]

    def block_rows(ref, blk):
        return ref[pl.ds(pl.multiple_of(blk * tb, tb), tb), :]

    def step(n, s):
        t = 1 - s
        i, j = pair(n)
        zr_ref[s] = lax.dot_general(block_rows(qb_ref, i), block_rows(kb_ref, j),
                                    (((1,), (1,)), ((), ())), preferred_element_type=F32)

        i, j = pair(n - 1)
        z = zr_ref[t] * scale + offset_ref[(i == j).astype(jnp.int32)]
        nl = jnp.maximum(z, 0.0) + jnp.log(1.0 + jnp.exp(-jnp.abs(z)))
        hi = nl.astype(BF16)
        z_ref[t] = z
        hi_ref[t] = hi
        lo_ref[t] = (nl - hi.astype(F32)).astype(BF16)

        i, j = pair(n - 2)
        rc = (jnp.dot(hi_ref[s], tri, preferred_element_type=F32)
              + jnp.dot(lo_ref[s], tri, preferred_element_type=F32))
        c = jnp.where(i == j, 0.0, c_ref[...])
        e_ref[s] = jnp.exp(z_ref[s] - rc - c).astype(BF16)
        c_ref[...] = c + rc[:, 0:1]

        i, j = pair(n - 3)
        acc = jnp.where(i == j, 0.0, acc_ref[...])
        new = acc + jnp.dot(e_ref[t], block_rows(vb_ref, j), preferred_element_type=F32)
        acc = jnp.where(n - 3 < npairs, new, acc)
        acc_ref[...] = acc
        o_ref[pl.ds(pl.multiple_of(i * tb, tb), tb), :] = acc.astype(o_ref.dtype)

    def trip(m, carry):
        for r in range(STEPS_PER_TRIP):
            step(STEPS_PER_TRIP * m + r, r % 2)
        return carry

    n_steps = npairs + 3
    lax.fori_loop(0, -(-n_steps // STEPS_PER_TRIP), trip, 0)


def sb_attention_prompt(q, k, v, bias, batch, seq, tb):
    nh, rows, dh = q.shape
    nblk = seq // tb
    pairs = [(i, j) for i in range(nblk) for j in range(i, -1, -1)]
    ij = jnp.asarray(pairs, dtype=jnp.int32).T
    blk = pl.BlockSpec((None, seq, dh), lambda b, h: (h, b, 0))
    smem = pl.BlockSpec(memory_space=pltpu.SMEM)
    stage_f32 = pltpu.VMEM((2, tb, tb), F32)
    stage_bf16 = pltpu.VMEM((2, tb, tb), BF16)
    return pl.pallas_call(
        functools.partial(_sb_prompt_kernel, tb=tb, scale=dh ** -0.5),
        grid=(batch, nh),
        in_specs=[smem, smem, blk, blk, blk],
        out_specs=blk,
        out_shape=jax.ShapeDtypeStruct((nh, rows, dh), BF16),
        scratch_shapes=[stage_f32, stage_f32, stage_bf16, stage_bf16, stage_bf16,
                        pltpu.VMEM((tb, 1), F32), pltpu.VMEM((tb, dh), F32), stage_f32],
        compiler_params=_params(2),
        name="sb_attention_prompt",
    )(ij, bias, q, k, v)


def _sb_sample_kernel(pt_ref, q_ref, kn_ref, vn_ref, bias_ref, *refs, n_pg, scale):
    del pt_ref
    nt, nh, dh = kn_ref.shape
    ngrp = nh // SUBLANES
    k_refs = refs[:n_pg * ngrp]
    v_refs = refs[n_pg * ngrp:2 * n_pg * ngrp]
    o_ref, qf_ref, c_ref, acc_ref = refs[2 * n_pg * ngrp:]

    def head_rows(page_refs, p, h):
        flat = page_refs[p * ngrp + h // SUBLANES].reshape(PAGE_SIZE * SUBLANES, dh)
        return flat[pl.ds(h % SUBLANES, PAGE_SIZE, stride=SUBLANES), :]

    rpad = qf_ref.shape[0] // nh
    g = pl.program_id(1)
    bias_col = bias_ref[...]

    @pl.when(g == 0)
    def _():
        for h in range(nh):
            qf_ref[h * rpad:(h + 1) * rpad, :] = q_ref[:, h * dh:(h + 1) * dh]
        rr = lax.broadcasted_iota(jnp.int32, (nh * rpad, nh), 0)
        hh = lax.broadcasted_iota(jnp.int32, (nh * rpad, nh), 1)
        rep = jnp.where(rr // rpad == hh, 1.0, 0.0).astype(BF16)
        step = lax.broadcasted_iota(jnp.int32, (nh * rpad, 1), 0) % rpad
        qr = qf_ref[...].astype(BF16).astype(F32)
        zs, lss, valids = [], [], []
        for s in range(nt):
            kr = jnp.dot(rep, kn_ref[s].astype(BF16), preferred_element_type=F32)
            z = jnp.sum(qr * kr, axis=1, keepdims=True) * scale + bias_col
            valid = step > s
            zs.append(z)
            valids.append(valid)
            lss.append(jnp.where(valid, _neg_softplus(z), 0.0))
        acc = jnp.zeros(acc_ref.shape, F32)
        rc = jnp.zeros((nh * rpad, 1), F32)
        for s in reversed(range(nt)):
            rc = rc + lss[s]
            a = jnp.where(valids[s], jnp.exp(zs[s] + rc), 0.0)
            vr = jnp.dot(rep, vn_ref[s].astype(BF16), preferred_element_type=F32)
            acc = acc + a.astype(BF16).astype(F32) * vr
        acc_ref[...] = acc
        c_ref[...] = rc

    bf16_values = lambda x: x.astype(BF16).astype(F32)
    width = n_pg * PAGE_SIZE
    zrows = []
    for h in range(nh):
        qh = bf16_values(qf_ref[h * rpad:(h + 1) * rpad, :])
        kh = jnp.concatenate([head_rows(k_refs, p, h) for p in range(n_pg)], axis=0)
        zrows.append(lax.dot_general(qh, kh, (((1,), (1,)), ((), ())), preferred_element_type=F32))
    z = jnp.concatenate(zrows, axis=0) * scale + bias_col
    ls = _neg_softplus(z)
    tb = 2 * PAGE_SIZE
    tri = _lower_tri(tb)
    c = c_ref[...]
    a_blocks = [None] * (width // tb)
    for blk in reversed(range(width // tb)):
        rc = _suffix_sum(ls[:, blk * tb:(blk + 1) * tb], tri)
        a_blocks[blk] = bf16_values(jnp.exp(z[:, blk * tb:(blk + 1) * tb] + rc + c))
        c = c + rc[:, 0:1]
    c_ref[...] = c
    a = jnp.concatenate(a_blocks, axis=1)
    orows = []
    for h in range(nh):
        vh = jnp.concatenate([head_rows(v_refs, p, h) for p in range(n_pg)], axis=0)
        orows.append(jnp.dot(a[h * rpad:(h + 1) * rpad, :], vh, preferred_element_type=F32))
    acc_ref[...] += jnp.concatenate(orows, axis=0)

    @pl.when(g == pl.num_programs(1) - 1)
    def _():
        for h in range(nh):
            o_ref[:, h * dh:(h + 1) * dh] = acc_ref[h * rpad:(h + 1) * rpad, :]


def sb_attention_sample(q, k_new, v_new, bias, cache_k, cache_v, page_table, n_pg):
    nb, nt, d_attn = q.shape
    nh, dh = k_new.shape[2], k_new.shape[3]
    n_pages = page_table.shape[1]
    rpad = SUBLANES
    q_pad = jnp.pad(q, ((0, 0), (0, rpad - nt), (0, 0)))
    bias_col = jnp.repeat(bias.astype(F32), rpad).reshape(nh * rpad, 1)
    n_steps = n_pages // n_pg

    ngrp = nh // SUBLANES
    cache_k = cache_k.reshape(-1, PAGE_SIZE, ngrp, SUBLANES, dh)
    cache_v = cache_v.reshape(-1, PAGE_SIZE, ngrp, SUBLANES, dh)

    def page_spec(p, grp):
        return pl.BlockSpec(
            (None, PAGE_SIZE, None, SUBLANES, dh),
            lambda b, g, pt: (pt[b, n_pages - (g + 1) * n_pg + p], 0, grp, 0, 0))

    page_specs = [page_spec(p, grp) for p in range(n_pg) for grp in range(ngrp)]

    per_seq3 = lambda b, g, pt: (b, 0, 0)
    per_seq4 = lambda b, g, pt: (b, 0, 0, 0)
    grid_spec = pltpu.PrefetchScalarGridSpec(
        num_scalar_prefetch=1,
        grid=(nb, n_steps),
        in_specs=[pl.BlockSpec((None, rpad, d_attn), per_seq3),
                  pl.BlockSpec((None, nt, nh, dh), per_seq4),
                  pl.BlockSpec((None, nt, nh, dh), per_seq4),
                  pl.BlockSpec((nh * rpad, 1), lambda b, g, pt: (0, 0))]
                 + page_specs * 2,
        out_specs=pl.BlockSpec((None, rpad, d_attn), per_seq3),
        scratch_shapes=[pltpu.VMEM((nh * rpad, dh), F32), pltpu.VMEM((nh * rpad, 1), F32),
                        pltpu.VMEM((nh * rpad, dh), F32)],
    )
    return pl.pallas_call(
        functools.partial(_sb_sample_kernel, n_pg=n_pg, scale=dh ** -0.5),
        grid_spec=grid_spec,
        out_shape=jax.ShapeDtypeStruct((nb, rpad, d_attn), F32),
        compiler_params=_params(2),
        name="sb_attention_sample",
    )(page_table, q_pad, k_new, v_new, bias_col,
      *([cache_k] * len(page_specs)), *([cache_v] * len(page_specs)))


def kernel(x_prompt, x_sample, state_conv, state_pool, cache_k, cache_v, page_table, ln_g, ln_b, mix_w_in, conv_w, pool_w, pool_scale, mix_w_out, attn_w_qkv, attn_w_o, attn_bias, ffn_w_gate, ffn_w_up, ffn_w_down):
    bp, seq, d = x_prompt.shape
    bs, ts_, _ = x_sample.shape
    mp, ms = bp * seq, bs * ts_
    past_len = page_table.shape[1] * PAGE_SIZE
    xp = x_prompt.reshape(mp, d)
    xs = x_sample.reshape(ms, d)
    dp = pool_scale.shape[1]
    dc = conv_w.shape[2]
    d_attn = attn_w_o.shape[1]
    dh = d_attn // N_HEADS
    ln_g = ln_g.reshape(-1, 1, d)
    ln_b = ln_b.reshape(-1, 1, d)

    def ffn(xp, xpb, xs, xsb, layer):
        hp, hs, wd = ffn_up(xpb, xsb, ffn_w_gate, ffn_w_up, ffn_w_down, layer, 1024, 512, f"ffn_up_l{layer}")
        return matmul_residual_ln(hp, hs, wd, 0, xp, xs, ln_g, ln_b, 2 * layer + 1, 256,
                                  f"ffn_down_ln_l{layer}")

    wp = pool_w[0].astype(BF16)
    sp = pool_scale[0].reshape(1, dp)
    hp, hs = matmul(xp, xs, mix_w_in, 0, 0, mix_w_in.shape[2], F32, 512, 1024, "mix_in")
    zp, conv_p, pool_p = mixer_prompt(hp, conv_w[0], wp, sp, bp, seq, 256)
    zs, conv_s, pool_s = mixer_sample(hs, state_conv[0], state_pool[0], conv_w[0], wp, sp, bs, ts_, past_len)
    xp, xpb, xs, xsb = matmul_residual_ln(zp, zs, mix_w_out, 0, xp, xs, ln_g, ln_b, 0, 256, "mix_out_ln")
    xp, xpb, xs, xsb = ffn(xp, xpb, xs, xsb, 0)

    bias = attn_bias[0]
    qh, qs = matmul_heads(xpb, xsb, attn_w_qkv, 0, 0, d_attn, dh, 1024, 1024, "q_proj", False)
    kp, kh, ks = matmul_heads(xpb, xsb, attn_w_qkv, 0, d_attn, d_attn, dh, 1024, 1024, "k_proj", True)
    vp, vh, vs = matmul_heads(xpb, xsb, attn_w_qkv, 0, 2 * d_attn, d_attn, dh, 1024, 1024, "v_proj", True)
    op = sb_attention_prompt(qh, kh, vh, bias, bp, seq, 256)
    ks4 = ks.reshape(bs, ts_, N_HEADS, dh)
    vs4 = vs.reshape(bs, ts_, N_HEADS, dh)
    o_pad = sb_attention_sample(qs.reshape(bs, ts_, d_attn), ks4, vs4, bias, cache_k[0], cache_v[0],
                                page_table, 8)
    os_ = o_pad[:, :ts_, :].reshape(ms, d_attn)
    xp, xpb, xs, xsb = matmul_residual_ln(op, os_, attn_w_o, 0, xp, xs, ln_g, ln_b, 2, 256, "attn_out_ln")
    xp, _, xs, _ = ffn(xp, xpb, xs, xsb, 1)

    return (xp.reshape(bp, seq, d), xs.reshape(bs, ts_, d),
            conv_p[None], conv_s[None], pool_p[None], pool_s[None],
            kp.reshape(1, bp, seq, N_HEADS, dh), vp.reshape(1, bp, seq, N_HEADS, dh),
            ks4[None], vs4[None])
```

```python
import functools

import jax
import jax.numpy as jnp
from jax import lax
from jax.experimental import pallas as pl
from jax.experimental.pallas import tpu as pltpu

F32 = jnp.float32
BF16 = jnp.bfloat16

CONV_WIDTH = 3
POOL_WINDOWS = (2, 4, 8, 16)
POOL_BUF = max(POOL_WINDOWS) - 1
N_HEADS = 16
PAGE_SIZE = 128
DEPTH = 2
ALPHA = (2.0 * DEPTH) ** 0.25
LN_EPS = 1e-5
MASKED_LOGIT = -1e30
STEPS_PER_TRIP = 20

VMEM_LIMIT_BYTES = 60 * 1024 * 1024
SUBLANES = 8


def _params(n_axes):
    return pltpu.CompilerParams(
        dimension_semantics=("arbitrary",) * n_axes,
        vmem_limit_bytes=VMEM_LIMIT_BYTES,
    )


def _dot(x, w):
    return jnp.dot(x.astype(BF16), w, preferred_element_type=F32)


def _mm_kernel(xm_ref, xe_ref, w_ref, om_ref, oe_ref, wb_ref):
    @pl.when(pl.program_id(1) == 0)
    def _():
        wb_ref[...] = w_ref[...].astype(BF16)
        oe_ref[...] = _dot(xe_ref[...], wb_ref[...]).astype(oe_ref.dtype)

    om_ref[...] = _dot(xm_ref[...], wb_ref[...]).astype(om_ref.dtype)


def matmul(xm, xe, w, layer, col0, n, out_dtype, tm, tn, name):
    m, k = xm.shape
    me = xe.shape[0]
    cb0 = col0 // tn
    return pl.pallas_call(
        _mm_kernel,
        grid=(n // tn, m // tm),
        in_specs=[pl.BlockSpec((tm, k), lambda j, i: (i, 0)),
                  pl.BlockSpec((me, k), lambda j, i: (0, 0)),
                  pl.BlockSpec((None, k, tn), lambda j, i: (layer, 0, cb0 + j))],
        out_specs=[pl.BlockSpec((tm, tn), lambda j, i: (i, j)),
                   pl.BlockSpec((me, tn), lambda j, i: (0, j))],
        out_shape=[jax.ShapeDtypeStruct((m, n), out_dtype), jax.ShapeDtypeStruct((me, n), out_dtype)],
        scratch_shapes=[pltpu.VMEM((k, tn), BF16)],
        compiler_params=_params(2),
        name=name,
    )(xm, xe, w)


def _mm_heads_kernel(xm_ref, xe_ref, w_ref, *refs, dh, row_major):
    om_ref = refs[0] if row_major else None
    oh_ref, oe_ref, wb_ref = refs[-3:]

    @pl.when(pl.program_id(1) == 0)
    def _():
        wb_ref[...] = w_ref[...].astype(BF16)
        oe_ref[...] = _dot(xe_ref[...], wb_ref[...])

    res = _dot(xm_ref[...], wb_ref[...])
    if row_major:
        om_ref[...] = res
    for h in range(oh_ref.shape[0]):
        oh_ref[h] = res[:, h * dh:(h + 1) * dh].astype(BF16)


def matmul_heads(xm, xe, w, layer, col0, n, dh, tm, tn, name, row_major):
    m, k = xm.shape
    me = xe.shape[0]
    cb0 = col0 // tn
    hpt = tn // dh
    out_specs = [pl.BlockSpec((hpt, tm, dh), lambda j, i: (j, i, 0)),
                 pl.BlockSpec((me, tn), lambda j, i: (0, j))]
    out_shape = [jax.ShapeDtypeStruct((n // dh, m, dh), BF16), jax.ShapeDtypeStruct((me, n), F32)]
    if row_major:
        out_specs.insert(0, pl.BlockSpec((tm, tn), lambda j, i: (i, j)))
        out_shape.insert(0, jax.ShapeDtypeStruct((m, n), F32))
    return pl.pallas_call(
        functools.partial(_mm_heads_kernel, dh=dh, row_major=row_major),
        grid=(n // tn, m // tm),
        in_specs=[pl.BlockSpec((tm, k), lambda j, i: (i, 0)),
                  pl.BlockSpec((me, k), lambda j, i: (0, 0)),
                  pl.BlockSpec((None, k, tn), lambda j, i: (layer, 0, cb0 + j))],
        out_specs=out_specs,
        out_shape=out_shape,
        scratch_shapes=[pltpu.VMEM((k, tn), BF16)],
        compiler_params=_params(2),
        name=name,
    )(xm, xe, w)


def _layer_norm_rows(s, g, b):
    mu = jnp.mean(s, axis=-1, keepdims=True)
    d = s - mu
    var = jnp.mean(d * d, axis=-1, keepdims=True)
    return d * lax.rsqrt(var + LN_EPS) * g + b


def _mm_ln_kernel(xm_ref, xe_ref, w_ref, rm_ref, re_ref, g_ref, b_ref,
                  ym_ref, ymb_ref, ye_ref, yeb_ref, *wb_refs):
    wb_ref = wb_refs[0] if wb_refs else w_ref

    def rows(x_ref, r_ref, y_ref, yb_ref):
        if len(x_ref.shape) == 3:
            x = jnp.concatenate([x_ref[h] for h in range(x_ref.shape[0])], axis=1)
        else:
            x = x_ref[...]
        y = _layer_norm_rows(ALPHA * r_ref[...] + _dot(x, wb_ref[...]), g_ref[...], b_ref[...])
        y_ref[...] = y
        yb_ref[...] = y.astype(BF16)

    @pl.when(pl.program_id(0) == 0)
    def _():
        if wb_refs:
            wb_ref[...] = w_ref[...].astype(BF16)
        rows(xe_ref, re_ref, ye_ref, yeb_ref)

    rows(xm_ref, rm_ref, ym_ref, ymb_ref)


def matmul_residual_ln(xm, xe, w, layer, rm, re, ln_g, ln_b, ln_row, tm, name):
    me, k = xe.shape
    m = xm.shape[-2]
    d = w.shape[-1]
    row = lambda i: (i, 0)
    const = lambda i: (0, 0)
    if xm.ndim == 3:
        xm_spec = pl.BlockSpec((xm.shape[0], tm, xm.shape[2]), lambda i: (0, i, 0))
    else:
        xm_spec = pl.BlockSpec((tm, k), row)
    if w.ndim == 2:
        w_spec = pl.BlockSpec((k, d), const, pipeline_mode=pl.Buffered(1))
        scratch = []
    else:
        w_spec = pl.BlockSpec((None, k, d), lambda i: (layer, 0, 0), pipeline_mode=pl.Buffered(1))
        scratch = [pltpu.VMEM((k, d), BF16)]
    ln_spec = pl.BlockSpec((None, 1, d), lambda i: (ln_row, 0, 0))
    return pl.pallas_call(
        _mm_ln_kernel,
        grid=(m // tm,),
        in_specs=[xm_spec, pl.BlockSpec((me, k), const), w_spec,
                  pl.BlockSpec((tm, d), row), pl.BlockSpec((me, d), const), ln_spec, ln_spec],
        out_specs=[pl.BlockSpec((tm, d), row), pl.BlockSpec((tm, d), row),
                   pl.BlockSpec((me, d), const), pl.BlockSpec((me, d), const)],
        out_shape=[jax.ShapeDtypeStruct((m, d), F32), jax.ShapeDtypeStruct((m, d), BF16),
                   jax.ShapeDtypeStruct((me, d), F32), jax.ShapeDtypeStruct((me, d), BF16)],
        scratch_shapes=scratch,
        compiler_params=_params(1),
        name=name,
    )(xm, xe, w, rm, re, ln_g, ln_b)


def _ffn_up_kernel(xm_ref, xe_ref, wg_ref, wu_ref, wd_ref, hm_ref, he_ref, wdb_ref, wgb_ref, wub_ref):
    def rows(x_ref, h_ref):
        x = x_ref[...]
        h_ref[...] = (jax.nn.silu(_dot(x, wgb_ref[...])) * _dot(x, wub_ref[...])).astype(h_ref.dtype)

    @pl.when(pl.program_id(1) == 0)
    def _():
        wgb_ref[...] = wg_ref[...].astype(BF16)
        wub_ref[...] = wu_ref[...].astype(BF16)
        rows(xe_ref, he_ref)

    wdb_ref[...] = wd_ref[...].astype(BF16)
    rows(xm_ref, hm_ref)


def ffn_up(xm, xe, wg, wu, wd, layer, tm, tf, name):
    m, k = xm.shape
    me = xe.shape[0]
    f = wg.shape[2]
    d = wd.shape[2]
    n_row_steps = m // tm
    slab = tf // n_row_steps
    col = lambda j, i: (layer, 0, j)
    return pl.pallas_call(
        _ffn_up_kernel,
        grid=(f // tf, m // tm),
        in_specs=[pl.BlockSpec((tm, k), lambda j, i: (i, 0)),
                  pl.BlockSpec((me, k), lambda j, i: (0, 0)),
                  pl.BlockSpec((None, k, tf), col), pl.BlockSpec((None, k, tf), col),
                  pl.BlockSpec((None, slab, d), lambda j, i: (layer, j * n_row_steps + i, 0))],
        out_specs=[pl.BlockSpec((tm, tf), lambda j, i: (i, j)),
                   pl.BlockSpec((me, tf), lambda j, i: (0, j)),
                   pl.BlockSpec((slab, d), lambda j, i: (j * n_row_steps + i, 0))],
        out_shape=[jax.ShapeDtypeStruct((m, f), BF16), jax.ShapeDtypeStruct((me, f), BF16),
                   jax.ShapeDtypeStruct((f, d), BF16)],
        scratch_shapes=[pltpu.VMEM((k, tf), BF16), pltpu.VMEM((k, tf), BF16)],
        compiler_params=_params(2),
        name=name,
    )(xm, xe, wg, wu, wd)


def _mixer_prompt_kernel(h_ref, wc_ref, wp_ref, sp_ref, z_ref, co_ref, po_ref, cbuf, pbuf, *, ts, dc):
    i = pl.program_id(1)
    ch, ph = SUBLANES, 2 * SUBLANES
    dp = pbuf.shape[1]
    pg = dp // len(POOL_WINDOWS)

    @pl.when(i == 0)
    def _():
        cbuf[0:ch, :] = jnp.zeros((ch, dc), F32)
        pbuf[0:ph, :] = jnp.zeros((ph, dp), F32)

    gb = h_ref[:, 0:dc]
    u = h_ref[:, dc:2 * dc] * h_ref[:, 2 * dc:3 * dc]
    pv = h_ref[:, 3 * dc:3 * dc + dp]
    cbuf[ch:ch + ts, :] = u
    pbuf[ph:ph + ts, :] = pv

    y = cbuf[ch - 2:ch - 2 + ts, :] * wc_ref[0:1, :]
    y = y + cbuf[ch - 1:ch - 1 + ts, :] * wc_ref[1:2, :]
    y = y + u * wc_ref[2:3, :]
    z_ref[:, 0:dc] = (gb * y).astype(z_ref.dtype)

    pos = i * ts + lax.broadcasted_iota(jnp.int32, (ts, 1), 0)
    for gi, w in enumerate(POOL_WINDOWS):
        lo = gi * pg
        cur = pv[:, lo:lo + pg]
        s = cur
        for back in range(1, w):
            s = s + pbuf[ph - back:ph - back + ts, lo:lo + pg]
        cnt = jnp.minimum(w, pos + 1).astype(F32)
        dlt = s / cnt - cur
        yg = jnp.dot(dlt.astype(BF16), wp_ref[gi], preferred_element_type=F32)
        z_ref[:, dc + lo:dc + lo + pg] = (yg * sp_ref[:, lo:lo + pg]).astype(z_ref.dtype)

    co_ref[...] = cbuf[ch + ts - (CONV_WIDTH - 1):ch + ts, :]
    po_ref[...] = pbuf[ph + ts - POOL_BUF:ph + ts, :]
    cbuf[0:ch, :] = cbuf[ts:ts + ch, :]
    pbuf[0:ph, :] = pbuf[ts:ts + ph, :]


def mixer_prompt(h, wc, wp, sp, batch, seq, ts):
    rows, width = h.shape
    dp = sp.shape[1]
    dc = (width - dp) // 3
    nt = seq // ts
    return pl.pallas_call(
        functools.partial(_mixer_prompt_kernel, ts=ts, dc=dc),
        grid=(batch, nt),
        in_specs=[pl.BlockSpec((ts, width), lambda b, i: (b * nt + i, 0)),
                  pl.BlockSpec(wc.shape, lambda b, i: (0, 0)),
                  pl.BlockSpec(wp.shape, lambda b, i: (0, 0, 0)),
                  pl.BlockSpec(sp.shape, lambda b, i: (0, 0))],
        out_specs=[pl.BlockSpec((ts, dc + dp), lambda b, i: (b * nt + i, 0)),
                   pl.BlockSpec((None, CONV_WIDTH - 1, dc), lambda b, i: (b, 0, 0)),
                   pl.BlockSpec((None, POOL_BUF, dp), lambda b, i: (b, 0, 0))],
        out_shape=[jax.ShapeDtypeStruct((rows, dc + dp), BF16),
                   jax.ShapeDtypeStruct((batch, CONV_WIDTH - 1, dc), F32),
                   jax.ShapeDtypeStruct((batch, POOL_BUF, dp), F32)],
        scratch_shapes=[pltpu.VMEM((ts + SUBLANES, dc), F32), pltpu.VMEM((ts + 2 * SUBLANES, dp), F32)],
        compiler_params=_params(2),
        name="mixer_prompt",
    )(h, wc, wp, sp)


def _mixer_sample_kernel(h_ref, sc_ref, spool_ref, wc_ref, wp_ref, sp_ref, z_ref, co_ref, po_ref,
                         *, nb, nt, dc, pos0):
    dp = sp_ref.shape[1]
    pg = dp // len(POOL_WINDOWS)
    nc = CONV_WIDTH - 1

    conv_ext = [sc_ref[j] for j in range(nc)]
    pool_ext = [spool_ref[j] for j in range(POOL_BUF)]
    gbs = []
    for t in range(nt):
        gbs.append(h_ref[t, :, 0:dc])
        conv_ext.append(h_ref[t, :, dc:2 * dc] * h_ref[t, :, 2 * dc:3 * dc])
        pool_ext.append(h_ref[t, :, 3 * dc:3 * dc + dp])

    for t in range(nt):
        y = conv_ext[t] * wc_ref[0:1, :]
        for kk in range(1, CONV_WIDTH):
            y = y + conv_ext[t + kk] * wc_ref[kk:kk + 1, :]
        z_ref[t, :, 0:dc] = gbs[t] * y
    for j in range(nc):
        co_ref[j] = conv_ext[nt + j]

    for gi, w in enumerate(POOL_WINDOWS):
        lo = gi * pg
        dls = []
        for t in range(nt):
            e = POOL_BUF + t
            s = pool_ext[e][:, lo:lo + pg]
            for back in range(1, w):
                s = s + pool_ext[e - back][:, lo:lo + pg]
            cnt = float(min(w, pos0 + t + 1))
            dls.append(s / cnt - pool_ext[e][:, lo:lo + pg])
        dl = jnp.concatenate(dls, axis=0).astype(BF16)
        yg = jnp.dot(dl, wp_ref[gi], preferred_element_type=F32) * sp_ref[:, lo:lo + pg]
        for t in range(nt):
            z_ref[t, :, dc + lo:dc + lo + pg] = yg[t * nb:(t + 1) * nb, :]
    for j in range(POOL_BUF):
        po_ref[j] = pool_ext[nt + j]


def mixer_sample(h, state_conv, state_pool, wc, wp, sp, nb, nt, pos0):
    rows, width = h.shape
    dp = sp.shape[1]
    dc = (width - dp) // 3
    nc = CONV_WIDTH - 1
    step_major = lambda a: jnp.swapaxes(a, 0, 1)
    z, conv_new, pool_new = pl.pallas_call(
        functools.partial(_mixer_sample_kernel, nb=nb, nt=nt, dc=dc, pos0=pos0),
        out_shape=[jax.ShapeDtypeStruct((nt, nb, dc + dp), F32),
                   jax.ShapeDtypeStruct((nc, nb, dc), F32),
                   jax.ShapeDtypeStruct((POOL_BUF, nb, dp), F32)],
        compiler_params=pltpu.CompilerParams(vmem_limit_bytes=VMEM_LIMIT_BYTES),
        name="mixer_sample",
    )(step_major(h.reshape(nb, nt, width)), step_major(state_conv), step_major(state_pool), wc, wp, sp)
    return step_major(z).reshape(rows, dc + dp), step_major(conv_new), step_major(pool_new)


def _neg_softplus(z):
    return -(jnp.maximum(z, 0.0) + jnp.log1p(jnp.exp(-jnp.abs(z))))


def _suffix_sum(ls, tri):
    hi = ls.astype(BF16)
    lo = (ls - hi.astype(F32)).astype(BF16)
    return (jnp.dot(hi, tri, preferred_element_type=F32)
            + jnp.dot(lo, tri, preferred_element_type=F32))


def _lower_tri(n):
    r = lax.broadcasted_iota(jnp.int32, (n, n), 0)
    c = lax.broadcasted_iota(jnp.int32, (n, n), 1)
    return jnp.where(r >= c, 1.0, 0.0).astype(BF16)


def _sb_prompt_kernel(ij_ref, bias_ref, qb_ref, kb_ref, vb_ref, o_ref,
                      zr_ref, z_ref, hi_ref, lo_ref, e_ref, c_ref, acc_ref, offset_ref, *, tb, scale):
    npairs = ij_ref.shape[1]
    bias = bias_ref[pl.program_id(1)]
    for ref in (zr_ref, z_ref, hi_ref, lo_ref, e_ref, c_ref, acc_ref):
        ref[...] = jnp.zeros(ref.shape, ref.dtype)
    tri = _lower_tri(tb)
    hidden = jnp.where(lax.broadcasted_iota(jnp.int32, (tb, tb), 1)
                       < lax.broadcasted_iota(jnp.int32, (tb, tb), 0), 0.0, MASKED_LOGIT)
    offset_ref[0] = jnp.full((tb, tb), bias, F32)
    offset_ref[1] = bias + hidden

    def pair(p):
        p = jnp.clip(p, 0, npairs - 1)
        return ij_ref[0, p], ij_ref[1, p]

    def block_rows(ref, blk):
        return ref[pl.ds(pl.multiple_of(blk * tb, tb), tb), :]

    def step(n, s):
        t = 1 - s
        i, j = pair(n)
        zr_ref[s] = lax.dot_general(block_rows(qb_ref, i), block_rows(kb_ref, j),
                                    (((1,), (1,)), ((), ())), preferred_element_type=F32)

        i, j = pair(n - 1)
        z = zr_ref[t] * scale + offset_ref[(i == j).astype(jnp.int32)]
        nl = jnp.maximum(z, 0.0) + jnp.log(1.0 + jnp.exp(-jnp.abs(z)))
        hi = nl.astype(BF16)
        z_ref[t] = z
        hi_ref[t] = hi
        lo_ref[t] = (nl - hi.astype(F32)).astype(BF16)

        i, j = pair(n - 2)
        rc = (jnp.dot(hi_ref[s], tri, preferred_element_type=F32)
              + jnp.dot(lo_ref[s], tri, preferred_element_type=F32))
        c = jnp.where(i == j, 0.0, c_ref[...])
        e_ref[s] = jnp.exp(z_ref[s] - rc - c).astype(BF16)
        c_ref[...] = c + rc[:, 0:1]

        i, j = pair(n - 3)
        acc = jnp.where(i == j, 0.0, acc_ref[...])
        new = acc + jnp.dot(e_ref[t], block_rows(vb_ref, j), preferred_element_type=F32)
        acc = jnp.where(n - 3 < npairs, new, acc)
        acc_ref[...] = acc
        o_ref[pl.ds(pl.multiple_of(i * tb, tb), tb), :] = acc.astype(o_ref.dtype)

    def trip(m, carry):
        for r in range(STEPS_PER_TRIP):
            step(STEPS_PER_TRIP * m + r, r % 2)
        return carry

    n_steps = npairs + 3
    lax.fori_loop(0, -(-n_steps // STEPS_PER_TRIP), trip, 0)


def sb_attention_prompt(q, k, v, bias, batch, seq, tb):
    nh, rows, dh = q.shape
    nblk = seq // tb
    pairs = [(i, j) for i in range(nblk) for j in range(i, -1, -1)]
    ij = jnp.asarray(pairs, dtype=jnp.int32).T
    blk = pl.BlockSpec((None, seq, dh), lambda b, h: (h, b, 0))
    smem = pl.BlockSpec(memory_space=pltpu.SMEM)
    stage_f32 = pltpu.VMEM((2, tb, tb), F32)
    stage_bf16 = pltpu.VMEM((2, tb, tb), BF16)
    return pl.pallas_call(
        functools.partial(_sb_prompt_kernel, tb=tb, scale=dh ** -0.5),
        grid=(batch, nh),
        in_specs=[smem, smem, blk, blk, blk],
        out_specs=blk,
        out_shape=jax.ShapeDtypeStruct((nh, rows, dh), BF16),
        scratch_shapes=[stage_f32, stage_f32, stage_bf16, stage_bf16, stage_bf16,
                        pltpu.VMEM((tb, 1), F32), pltpu.VMEM((tb, dh), F32), stage_f32],
        compiler_params=_params(2),
        name="sb_attention_prompt",
    )(ij, bias, q, k, v)


def _sb_sample_kernel(pt_ref, q_ref, kn_ref, vn_ref, bias_ref, *refs, n_pg, scale):
    del pt_ref
    nt, nh, dh = kn_ref.shape
    ngrp = nh // SUBLANES
    k_refs = refs[:n_pg * ngrp]
    v_refs = refs[n_pg * ngrp:2 * n_pg * ngrp]
    o_ref, qf_ref, c_ref, acc_ref = refs[2 * n_pg * ngrp:]

    def head_rows(page_refs, p, h):
        flat = page_refs[p * ngrp + h // SUBLANES].reshape(PAGE_SIZE * SUBLANES, dh)
        return flat[pl.ds(h % SUBLANES, PAGE_SIZE, stride=SUBLANES), :]

    rpad = qf_ref.shape[0] // nh
    g = pl.program_id(1)
    bias_col = bias_ref[...]

    @pl.when(g == 0)
    def _():
        for h in range(nh):
            qf_ref[h * rpad:(h + 1) * rpad, :] = q_ref[:, h * dh:(h + 1) * dh]
        rr = lax.broadcasted_iota(jnp.int32, (nh * rpad, nh), 0)
        hh = lax.broadcasted_iota(jnp.int32, (nh * rpad, nh), 1)
        rep = jnp.where(rr // rpad == hh, 1.0, 0.0).astype(BF16)
        step = lax.broadcasted_iota(jnp.int32, (nh * rpad, 1), 0) % rpad
        qr = qf_ref[...].astype(BF16).astype(F32)
        zs, lss, valids = [], [], []
        for s in range(nt):
            kr = jnp.dot(rep, kn_ref[s].astype(BF16), preferred_element_type=F32)
            z = jnp.sum(qr * kr, axis=1, keepdims=True) * scale + bias_col
            valid = step > s
            zs.append(z)
            valids.append(valid)
            lss.append(jnp.where(valid, _neg_softplus(z), 0.0))
        acc = jnp.zeros(acc_ref.shape, F32)
        rc = jnp.zeros((nh * rpad, 1), F32)
        for s in reversed(range(nt)):
            rc = rc + lss[s]
            a = jnp.where(valids[s], jnp.exp(zs[s] + rc), 0.0)
            vr = jnp.dot(rep, vn_ref[s].astype(BF16), preferred_element_type=F32)
            acc = acc + a.astype(BF16).astype(F32) * vr
        acc_ref[...] = acc
        c_ref[...] = rc

    bf16_values = lambda x: x.astype(BF16).astype(F32)
    width = n_pg * PAGE_SIZE
    zrows = []
    for h in range(nh):
        qh = bf16_values(qf_ref[h * rpad:(h + 1) * rpad, :])
        kh = jnp.concatenate([head_rows(k_refs, p, h) for p in range(n_pg)], axis=0)
        zrows.append(lax.dot_general(qh, kh, (((1,), (1,)), ((), ())), preferred_element_type=F32))
    z = jnp.concatenate(zrows, axis=0) * scale + bias_col
    ls = _neg_softplus(z)
    tb = 2 * PAGE_SIZE
    tri = _lower_tri(tb)
    c = c_ref[...]
    a_blocks = [None] * (width // tb)
    for blk in reversed(range(width // tb)):
        rc = _suffix_sum(ls[:, blk * tb:(blk + 1) * tb], tri)
        a_blocks[blk] = bf16_values(jnp.exp(z[:, blk * tb:(blk + 1) * tb] + rc + c))
        c = c + rc[:, 0:1]
    c_ref[...] = c
    a = jnp.concatenate(a_blocks, axis=1)
    orows = []
    for h in range(nh):
        vh = jnp.concatenate([head_rows(v_refs, p, h) for p in range(n_pg)], axis=0)
        orows.append(jnp.dot(a[h * rpad:(h + 1) * rpad, :], vh, preferred_element_type=F32))
    acc_ref[...] += jnp.concatenate(orows, axis=0)

    @pl.when(g == pl.num_programs(1) - 1)
    def _():
        for h in range(nh):
            o_ref[:, h * dh:(h + 1) * dh] = acc_ref[h * rpad:(h + 1) * rpad, :]


def sb_attention_sample(q, k_new, v_new, bias, cache_k, cache_v, page_table, n_pg):
    nb, nt, d_attn = q.shape
    nh, dh = k_new.shape[2], k_new.shape[3]
    n_pages = page_table.shape[1]
    rpad = SUBLANES
    q_pad = jnp.pad(q, ((0, 0), (0, rpad - nt), (0, 0)))
    bias_col = jnp.repeat(bias.astype(F32), rpad).reshape(nh * rpad, 1)
    n_steps = n_pages // n_pg

    ngrp = nh // SUBLANES
    cache_k = cache_k.reshape(-1, PAGE_SIZE, ngrp, SUBLANES, dh)
    cache_v = cache_v.reshape(-1, PAGE_SIZE, ngrp, SUBLANES, dh)

    def page_spec(p, grp):
        return pl.BlockSpec(
            (None, PAGE_SIZE, None, SUBLANES, dh),
            lambda b, g, pt: (pt[b, n_pages - (g + 1) * n_pg + p], 0, grp, 0, 0))

    page_specs = [page_spec(p, grp) for p in range(n_pg) for grp in range(ngrp)]

    per_seq3 = lambda b, g, pt: (b, 0, 0)
    per_seq4 = lambda b, g, pt: (b, 0, 0, 0)
    grid_spec = pltpu.PrefetchScalarGridSpec(
        num_scalar_prefetch=1,
        grid=(nb, n_steps),
        in_specs=[pl.BlockSpec((None, rpad, d_attn), per_seq3),
                  pl.BlockSpec((None, nt, nh, dh), per_seq4),
                  pl.BlockSpec((None, nt, nh, dh), per_seq4),
                  pl.BlockSpec((nh * rpad, 1), lambda b, g, pt: (0, 0))]
                 + page_specs * 2,
        out_specs=pl.BlockSpec((None, rpad, d_attn), per_seq3),
        scratch_shapes=[pltpu.VMEM((nh * rpad, dh), F32), pltpu.VMEM((nh * rpad, 1), F32),
                        pltpu.VMEM((nh * rpad, dh), F32)],
    )
    return pl.pallas_call(
        functools.partial(_sb_sample_kernel, n_pg=n_pg, scale=dh ** -0.5),
        grid_spec=grid_spec,
        out_shape=jax.ShapeDtypeStruct((nb, rpad, d_attn), F32),
        compiler_params=_params(2),
        name="sb_attention_sample",
    )(page_table, q_pad, k_new, v_new, bias_col,
      *([cache_k] * len(page_specs)), *([cache_v] * len(page_specs)))


def kernel(x_prompt, x_sample, state_conv, state_pool, cache_k, cache_v, page_table, ln_g, ln_b, mix_w_in, conv_w, pool_w, pool_scale, mix_w_out, attn_w_qkv, attn_w_o, attn_bias, ffn_w_gate, ffn_w_up, ffn_w_down):
    bp, seq, d = x_prompt.shape
    bs, ts_, _ = x_sample.shape
    mp, ms = bp * seq, bs * ts_
    past_len = page_table.shape[1] * PAGE_SIZE
    xp = x_prompt.reshape(mp, d)
    xs = x_sample.reshape(ms, d)
    dp = pool_scale.shape[1]
    dc = conv_w.shape[2]
    d_attn = attn_w_o.shape[1]
    dh = d_attn // N_HEADS
    ln_g = ln_g.reshape(-1, 1, d)
    ln_b = ln_b.reshape(-1, 1, d)

    def ffn(xp, xpb, xs, xsb, layer):
        hp, hs, wd = ffn_up(xpb, xsb, ffn_w_gate, ffn_w_up, ffn_w_down, layer, 2048, 256, f"ffn_up_l{layer}")
        return matmul_residual_ln(hp, hs, wd, 0, xp, xs, ln_g, ln_b, 2 * layer + 1, 256,
                                  f"ffn_down_ln_l{layer}")

    wp = pool_w[0].astype(BF16)
    sp = pool_scale[0].reshape(1, dp)
    hp, hs = matmul(xp, xs, mix_w_in, 0, 0, mix_w_in.shape[2], F32, 1024, 1024, "mix_in")
    zp, conv_p, pool_p = mixer_prompt(hp, conv_w[0], wp, sp, bp, seq, 256)
    zs, conv_s, pool_s = mixer_sample(hs, state_conv[0], state_pool[0], conv_w[0], wp, sp, bs, ts_, past_len)
    xp, xpb, xs, xsb = matmul_residual_ln(zp, zs, mix_w_out, 0, xp, xs, ln_g, ln_b, 0, 512, "mix_out_ln")
    xp, xpb, xs, xsb = ffn(xp, xpb, xs, xsb, 0)

    bias = attn_bias[0]
    qh, qs = matmul_heads(xpb, xsb, attn_w_qkv, 0, 0, d_attn, dh, 1024, 1024, "q_proj", False)
    kp, kh, ks = matmul_heads(xpb, xsb, attn_w_qkv, 0, d_attn, d_attn, dh, 1024, 1024, "k_proj", True)
    vp, vh, vs = matmul_heads(xpb, xsb, attn_w_qkv, 0, 2 * d_attn, d_attn, dh, 1024, 1024, "v_proj", True)
    op = sb_attention_prompt(qh, kh, vh, bias, bp, seq, 256)
    ks4 = ks.reshape(bs, ts_, N_HEADS, dh)
    vs4 = vs.reshape(bs, ts_, N_HEADS, dh)
    o_pad = sb_attention_sample(qs.reshape(bs, ts_, d_attn), ks4, vs4, bias, cache_k[0], cache_v[0],
                                page_table, 8)
    os_ = o_pad[:, :ts_, :].reshape(ms, d_attn)
    xp, xpb, xs, xsb = matmul_residual_ln(op, os_, attn_w_o, 0, xp, xs, ln_g, ln_b, 2, 512, "attn_out_ln")
    xp, _, xs, _ = ffn(xp, xpb, xs, xsb, 1)

    return (xp.reshape(bp, seq, d), xs.reshape(bs, ts_, d),
            conv_p[None], conv_s[None], pool_p[None], pool_s[None],
            kp.reshape(1, bp, seq, N_HEADS, dh), vp.reshape(1, bp, seq, N_HEADS, dh),
            ks4[None], vs4[None])
```

```python
import functools

import jax
import jax.numpy as jnp
from jax import lax
from jax.experimental import pallas as pl
from jax.experimental.pallas import tpu as pltpu

F32 = jnp.float32
BF16 = jnp.bfloat16

CONV_WIDTH = 3
POOL_WINDOWS = (2, 4, 8, 16)
POOL_BUF = max(POOL_WINDOWS) - 1
N_HEADS = 16
PAGE_SIZE = 128
DEPTH = 2
ALPHA = (2.0 * DEPTH) ** 0.25
LN_EPS = 1e-5
MASKED_LOGIT = -1e30
STEPS_PER_TRIP = 40

VMEM_LIMIT_BYTES = 60 * 1024 * 1024
SUBLANES = 8


def _params(n_axes):
    return pltpu.CompilerParams(
        dimension_semantics=("arbitrary",) * n_axes,
        vmem_limit_bytes=VMEM_LIMIT_BYTES,
    )


def _dot(x, w):
    return jnp.dot(x.astype(BF16), w, preferred_element_type=F32)


def _mm_kernel(xm_ref, xe_ref, w_ref, om_ref, oe_ref, wb_ref):
    @pl.when(pl.program_id(1) == 0)
    def _():
        wb_ref[...] = w_ref[...].astype(BF16)
        oe_ref[...] = _dot(xe_ref[...], wb_ref[...]).astype(oe_ref.dtype)

    om_ref[...] = _dot(xm_ref[...], wb_ref[...]).astype(om_ref.dtype)


def matmul(xm, xe, w, layer, col0, n, out_dtype, tm, tn, name):
    m, k = xm.shape
    me = xe.shape[0]
    cb0 = col0 // tn
    return pl.pallas_call(
        _mm_kernel,
        grid=(n // tn, m // tm),
        in_specs=[pl.BlockSpec((tm, k), lambda j, i: (i, 0)),
                  pl.BlockSpec((me, k), lambda j, i: (0, 0)),
                  pl.BlockSpec((None, k, tn), lambda j, i: (layer, 0, cb0 + j))],
        out_specs=[pl.BlockSpec((tm, tn), lambda j, i: (i, j)),
                   pl.BlockSpec((me, tn), lambda j, i: (0, j))],
        out_shape=[jax.ShapeDtypeStruct((m, n), out_dtype), jax.ShapeDtypeStruct((me, n), out_dtype)],
        scratch_shapes=[pltpu.VMEM((k, tn), BF16)],
        compiler_params=_params(2),
        name=name,
    )(xm, xe, w)


def _mm_heads_kernel(xm_ref, xe_ref, w_ref, *refs, dh, row_major):
    om_ref = refs[0] if row_major else None
    oh_ref, oe_ref, wb_ref = refs[-3:]

    @pl.when(pl.program_id(1) == 0)
    def _():
        wb_ref[...] = w_ref[...].astype(BF16)
        oe_ref[...] = _dot(xe_ref[...], wb_ref[...])

    res = _dot(xm_ref[...], wb_ref[...])
    if row_major:
        om_ref[...] = res
    for h in range(oh_ref.shape[0]):
        oh_ref[h] = res[:, h * dh:(h + 1) * dh].astype(BF16)


def matmul_heads(xm, xe, w, layer, col0, n, dh, tm, tn, name, row_major):
    m, k = xm.shape
    me = xe.shape[0]
    cb0 = col0 // tn
    hpt = tn // dh
    out_specs = [pl.BlockSpec((hpt, tm, dh), lambda j, i: (j, i, 0)),
                 pl.BlockSpec((me, tn), lambda j, i: (0, j))]
    out_shape = [jax.ShapeDtypeStruct((n // dh, m, dh), BF16), jax.ShapeDtypeStruct((me, n), F32)]
    if row_major:
        out_specs.insert(0, pl.BlockSpec((tm, tn), lambda j, i: (i, j)))
        out_shape.insert(0, jax.ShapeDtypeStruct((m, n), F32))
    return pl.pallas_call(
        functools.partial(_mm_heads_kernel, dh=dh, row_major=row_major),
        grid=(n // tn, m // tm),
        in_specs=[pl.BlockSpec((tm, k), lambda j, i: (i, 0)),
                  pl.BlockSpec((me, k), lambda j, i: (0, 0)),
                  pl.BlockSpec((None, k, tn), lambda j, i: (layer, 0, cb0 + j))],
        out_specs=out_specs,
        out_shape=out_shape,
        scratch_shapes=[pltpu.VMEM((k, tn), BF16)],
        compiler_params=_params(2),
        name=name,
    )(xm, xe, w)


def _layer_norm_rows(s, g, b):
    mu = jnp.mean(s, axis=-1, keepdims=True)
    d = s - mu
    var = jnp.mean(d * d, axis=-1, keepdims=True)
    return d * lax.rsqrt(var + LN_EPS) * g + b


def _mm_ln_kernel(xm_ref, xe_ref, w_ref, rm_ref, re_ref, g_ref, b_ref,
                  ym_ref, ymb_ref, ye_ref, yeb_ref, *wb_refs):
    wb_ref = wb_refs[0] if wb_refs else w_ref

    def rows(x_ref, r_ref, y_ref, yb_ref):
        if len(x_ref.shape) == 3:
            x = jnp.concatenate([x_ref[h] for h in range(x_ref.shape[0])], axis=1)
        else:
            x = x_ref[...]
        y = _layer_norm_rows(ALPHA * r_ref[...] + _dot(x, wb_ref[...]), g_ref[...], b_ref[...])
        y_ref[...] = y
        yb_ref[...] = y.astype(BF16)

    @pl.when(pl.program_id(0) == 0)
    def _():
        if wb_refs:
            wb_ref[...] = w_ref[...].astype(BF16)
        rows(xe_ref, re_ref, ye_ref, yeb_ref)

    rows(xm_ref, rm_ref, ym_ref, ymb_ref)


def matmul_residual_ln(xm, xe, w, layer, rm, re, ln_g, ln_b, ln_row, tm, name):
    me, k = xe.shape
    m = xm.shape[-2]
    d = w.shape[-1]
    row = lambda i: (i, 0)
    const = lambda i: (0, 0)
    if xm.ndim == 3:
        xm_spec = pl.BlockSpec((xm.shape[0], tm, xm.shape[2]), lambda i: (0, i, 0))
    else:
        xm_spec = pl.BlockSpec((tm, k), row)
    if w.ndim == 2:
        w_spec = pl.BlockSpec((k, d), const, pipeline_mode=pl.Buffered(1))
        scratch = []
    else:
        w_spec = pl.BlockSpec((None, k, d), lambda i: (layer, 0, 0), pipeline_mode=pl.Buffered(1))
        scratch = [pltpu.VMEM((k, d), BF16)]
    ln_spec = pl.BlockSpec((None, 1, d), lambda i: (ln_row, 0, 0))
    return pl.pallas_call(
        _mm_ln_kernel,
        grid=(m // tm,),
        in_specs=[xm_spec, pl.BlockSpec((me, k), const), w_spec,
                  pl.BlockSpec((tm, d), row), pl.BlockSpec((me, d), const), ln_spec, ln_spec],
        out_specs=[pl.BlockSpec((tm, d), row), pl.BlockSpec((tm, d), row),
                   pl.BlockSpec((me, d), const), pl.BlockSpec((me, d), const)],
        out_shape=[jax.ShapeDtypeStruct((m, d), F32), jax.ShapeDtypeStruct((m, d), BF16),
                   jax.ShapeDtypeStruct((me, d), F32), jax.ShapeDtypeStruct((me, d), BF16)],
        scratch_shapes=scratch,
        compiler_params=_params(1),
        name=name,
    )(xm, xe, w, rm, re, ln_g, ln_b)


def _ffn_up_kernel(xm_ref, xe_ref, wg_ref, wu_ref, wd_ref, hm_ref, he_ref, wdb_ref, wgb_ref, wub_ref):
    def rows(x_ref, h_ref):
        x = x_ref[...]
        h_ref[...] = (jax.nn.silu(_dot(x, wgb_ref[...])) * _dot(x, wub_ref[...])).astype(h_ref.dtype)

    @pl.when(pl.program_id(1) == 0)
    def _():
        wgb_ref[...] = wg_ref[...].astype(BF16)
        wub_ref[...] = wu_ref[...].astype(BF16)
        rows(xe_ref, he_ref)

    wdb_ref[...] = wd_ref[...].astype(BF16)
    rows(xm_ref, hm_ref)


def ffn_up(xm, xe, wg, wu, wd, layer, tm, tf, name):
    m, k = xm.shape
    me = xe.shape[0]
    f = wg.shape[2]
    d = wd.shape[2]
    n_row_steps = m // tm
    slab = tf // n_row_steps
    col = lambda j, i: (layer, 0, j)
    return pl.pallas_call(
        _ffn_up_kernel,
        grid=(f // tf, m // tm),
        in_specs=[pl.BlockSpec((tm, k), lambda j, i: (i, 0)),
                  pl.BlockSpec((me, k), lambda j, i: (0, 0)),
                  pl.BlockSpec((None, k, tf), col), pl.BlockSpec((None, k, tf), col),
                  pl.BlockSpec((None, slab, d), lambda j, i: (layer, j * n_row_steps + i, 0))],
        out_specs=[pl.BlockSpec((tm, tf), lambda j, i: (i, j)),
                   pl.BlockSpec((me, tf), lambda j, i: (0, j)),
                   pl.BlockSpec((slab, d), lambda j, i: (j * n_row_steps + i, 0))],
        out_shape=[jax.ShapeDtypeStruct((m, f), BF16), jax.ShapeDtypeStruct((me, f), BF16),
                   jax.ShapeDtypeStruct((f, d), BF16)],
        scratch_shapes=[pltpu.VMEM((k, tf), BF16), pltpu.VMEM((k, tf), BF16)],
        compiler_params=_params(2),
        name=name,
    )(xm, xe, wg, wu, wd)


def _mixer_prompt_kernel(h_ref, wc_ref, wp_ref, sp_ref, z_ref, co_ref, po_ref, cbuf, pbuf, *, ts, dc):
    i = pl.program_id(1)
    ch, ph = SUBLANES, 2 * SUBLANES
    dp = pbuf.shape[1]
    pg = dp // len(POOL_WINDOWS)

    @pl.when(i == 0)
    def _():
        cbuf[0:ch, :] = jnp.zeros((ch, dc), F32)
        pbuf[0:ph, :] = jnp.zeros((ph, dp), F32)

    gb = h_ref[:, 0:dc]
    u = h_ref[:, dc:2 * dc] * h_ref[:, 2 * dc:3 * dc]
    pv = h_ref[:, 3 * dc:3 * dc + dp]
    cbuf[ch:ch + ts, :] = u
    pbuf[ph:ph + ts, :] = pv

    y = cbuf[ch - 2:ch - 2 + ts, :] * wc_ref[0:1, :]
    y = y + cbuf[ch - 1:ch - 1 + ts, :] * wc_ref[1:2, :]
    y = y + u * wc_ref[2:3, :]
    z_ref[:, 0:dc] = (gb * y).astype(z_ref.dtype)

    pos = i * ts + lax.broadcasted_iota(jnp.int32, (ts, 1), 0)
    for gi, w in enumerate(POOL_WINDOWS):
        lo = gi * pg
        cur = pv[:, lo:lo + pg]
        s = cur
        for back in range(1, w):
            s = s + pbuf[ph - back:ph - back + ts, lo:lo + pg]
        cnt = jnp.minimum(w, pos + 1).astype(F32)
        dlt = s / cnt - cur
        yg = jnp.dot(dlt.astype(BF16), wp_ref[gi], preferred_element_type=F32)
        z_ref[:, dc + lo:dc + lo + pg] = (yg * sp_ref[:, lo:lo + pg]).astype(z_ref.dtype)

    co_ref[...] = cbuf[ch + ts - (CONV_WIDTH - 1):ch + ts, :]
    po_ref[...] = pbuf[ph + ts - POOL_BUF:ph + ts, :]
    cbuf[0:ch, :] = cbuf[ts:ts + ch, :]
    pbuf[0:ph, :] = pbuf[ts:ts + ph, :]


def mixer_prompt(h, wc, wp, sp, batch, seq, ts):
    rows, width = h.shape
    dp = sp.shape[1]
    dc = (width - dp) // 3
    nt = seq // ts
    return pl.pallas_call(
        functools.partial(_mixer_prompt_kernel, ts=ts, dc=dc),
        grid=(batch, nt),
        in_specs=[pl.BlockSpec((ts, width), lambda b, i: (b * nt + i, 0)),
                  pl.BlockSpec(wc.shape, lambda b, i: (0, 0)),
                  pl.BlockSpec(wp.shape, lambda b, i: (0, 0, 0)),
                  pl.BlockSpec(sp.shape, lambda b, i: (0, 0))],
        out_specs=[pl.BlockSpec((ts, dc + dp), lambda b, i: (b * nt + i, 0)),
                   pl.BlockSpec((None, CONV_WIDTH - 1, dc), lambda b, i: (b, 0, 0)),
                   pl.BlockSpec((None, POOL_BUF, dp), lambda b, i: (b, 0, 0))],
        out_shape=[jax.ShapeDtypeStruct((rows, dc + dp), BF16),
                   jax.ShapeDtypeStruct((batch, CONV_WIDTH - 1, dc), F32),
                   jax.ShapeDtypeStruct((batch, POOL_BUF, dp), F32)],
        scratch_shapes=[pltpu.VMEM((ts + SUBLANES, dc), F32), pltpu.VMEM((ts + 2 * SUBLANES, dp), F32)],
        compiler_params=_params(2),
        name="mixer_prompt",
    )(h, wc, wp, sp)


def _mixer_sample_kernel(h_ref, sc_ref, spool_ref, wc_ref, wp_ref, sp_ref, z_ref, co_ref, po_ref,
                         *, nb, nt, dc, pos0):
    dp = sp_ref.shape[1]
    pg = dp // len(POOL_WINDOWS)
    nc = CONV_WIDTH - 1

    conv_ext = [sc_ref[j] for j in range(nc)]
    pool_ext = [spool_ref[j] for j in range(POOL_BUF)]
    gbs = []
    for t in range(nt):
        gbs.append(h_ref[t, :, 0:dc])
        conv_ext.append(h_ref[t, :, dc:2 * dc] * h_ref[t, :, 2 * dc:3 * dc])
        pool_ext.append(h_ref[t, :, 3 * dc:3 * dc + dp])

    for t in range(nt):
        y = conv_ext[t] * wc_ref[0:1, :]
        for kk in range(1, CONV_WIDTH):
            y = y + conv_ext[t + kk] * wc_ref[kk:kk + 1, :]
        z_ref[t, :, 0:dc] = gbs[t] * y
    for j in range(nc):
        co_ref[j] = conv_ext[nt + j]

    for gi, w in enumerate(POOL_WINDOWS):
        lo = gi * pg
        dls = []
        for t in range(nt):
            e = POOL_BUF + t
            s = pool_ext[e][:, lo:lo + pg]
            for back in range(1, w):
                s = s + pool_ext[e - back][:, lo:lo + pg]
            cnt = float(min(w, pos0 + t + 1))
            dls.append(s / cnt - pool_ext[e][:, lo:lo + pg])
        dl = jnp.concatenate(dls, axis=0).astype(BF16)
        yg = jnp.dot(dl, wp_ref[gi], preferred_element_type=F32) * sp_ref[:, lo:lo + pg]
        for t in range(nt):
            z_ref[t, :, dc + lo:dc + lo + pg] = yg[t * nb:(t + 1) * nb, :]
    for j in range(POOL_BUF):
        po_ref[j] = pool_ext[nt + j]


def mixer_sample(h, state_conv, state_pool, wc, wp, sp, nb, nt, pos0):
    rows, width = h.shape
    dp = sp.shape[1]
    dc = (width - dp) // 3
    nc = CONV_WIDTH - 1
    step_major = lambda a: jnp.swapaxes(a, 0, 1)
    z, conv_new, pool_new = pl.pallas_call(
        functools.partial(_mixer_sample_kernel, nb=nb, nt=nt, dc=dc, pos0=pos0),
        out_shape=[jax.ShapeDtypeStruct((nt, nb, dc + dp), F32),
                   jax.ShapeDtypeStruct((nc, nb, dc), F32),
                   jax.ShapeDtypeStruct((POOL_BUF, nb, dp), F32)],
        compiler_params=pltpu.CompilerParams(vmem_limit_bytes=VMEM_LIMIT_BYTES),
        name="mixer_sample",
    )(step_major(h.reshape(nb, nt, width)), step_major(state_conv), step_major(state_pool), wc, wp, sp)
    return step_major(z).reshape(rows, dc + dp), step_major(conv_new), step_major(pool_new)


def _neg_softplus(z):
    return -(jnp.maximum(z, 0.0) + jnp.log1p(jnp.exp(-jnp.abs(z))))


def _suffix_sum(ls, tri):
    hi = ls.astype(BF16)
    lo = (ls - hi.astype(F32)).astype(BF16)
    return (jnp.dot(hi, tri, preferred_element_type=F32)
            + jnp.dot(lo, tri, preferred_element_type=F32))


def _lower_tri(n):
    r = lax.broadcasted_iota(jnp.int32, (n, n), 0)
    c = lax.broadcasted_iota(jnp.int32, (n, n), 1)
    return jnp.where(r >= c, 1.0, 0.0).astype(BF16)


def _sb_prompt_kernel(ij_ref, bias_ref, qb_ref, kb_ref, vb_ref, o_ref,
                      zr_ref, z_ref, hi_ref, lo_ref, e_ref, c_ref, acc_ref, offset_ref, *, tb, scale):
    npairs = ij_ref.shape[1]
    bias = bias_ref[pl.program_id(1)]
    for ref in (zr_ref, z_ref, hi_ref, lo_ref, e_ref, c_ref, acc_ref):
        ref[...] = jnp.zeros(ref.shape, ref.dtype)
    tri = _lower_tri(tb)
    hidden = jnp.where(lax.broadcasted_iota(jnp.int32, (tb, tb), 1)
                       < lax.broadcasted_iota(jnp.int32, (tb, tb), 0), 0.0, MASKED_LOGIT)
    offset_ref[0] = jnp.full((tb, tb), bias, F32)
    offset_ref[1] = bias + hidden

    def pair(p):
        p = jnp.clip(p, 0, npairs - 1)
        return ij_ref[0, p], ij_ref[1, p]

    def block_rows(ref, blk):
        return ref[pl.ds(pl.multiple_of(blk * tb, tb), tb), :]

    def step(n, s):
        t = 1 - s
        i, j = pair(n)
        zr_ref[s] = lax.dot_general(block_rows(qb_ref, i), block_rows(kb_ref, j),
                                    (((1,), (1,)), ((), ())), preferred_element_type=F32)

        i, j = pair(n - 1)
        z = zr_ref[t] * scale + offset_ref[(i == j).astype(jnp.int32)]
        nl = jnp.maximum(z, 0.0) + jnp.log(1.0 + jnp.exp(-jnp.abs(z)))
        hi = nl.astype(BF16)
        z_ref[t] = z
        hi_ref[t] = hi
        lo_ref[t] = (nl - hi.astype(F32)).astype(BF16)

        i, j = pair(n - 2)
        rc = (jnp.dot(hi_ref[s], tri, preferred_element_type=F32)
              + jnp.dot(lo_ref[s], tri, preferred_element_type=F32))
        c = jnp.where(i == j, 0.0, c_ref[...])
        e_ref[s] = jnp.exp(z_ref[s] - rc - c).astype(BF16)
        c_ref[...] = c + rc[:, 0:1]

        i, j = pair(n - 3)
        acc = jnp.where(i == j, 0.0, acc_ref[...])
        new = acc + jnp.dot(e_ref[t], block_rows(vb_ref, j), preferred_element_type=F32)
        acc = jnp.where(n - 3 < npairs, new, acc)
        acc_ref[...] = acc
        o_ref[pl.ds(pl.multiple_of(i * tb, tb), tb), :] = acc.astype(o_ref.dtype)

    def trip(m, carry):
        for r in range(STEPS_PER_TRIP):
            step(STEPS_PER_TRIP * m + r, r % 2)
        return carry

    n_steps = npairs + 3
    lax.fori_loop(0, -(-n_steps // STEPS_PER_TRIP), trip, 0)


def sb_attention_prompt(q, k, v, bias, batch, seq, tb):
    nh, rows, dh = q.shape
    nblk = seq // tb
    pairs = [(i, j) for i in range(nblk) for j in range(i, -1, -1)]
    ij = jnp.asarray(pairs, dtype=jnp.int32).T
    blk = pl.BlockSpec((None, seq, dh), lambda b, h: (h, b, 0))
    smem = pl.BlockSpec(memory_space=pltpu.SMEM)
    stage_f32 = pltpu.VMEM((2, tb, tb), F32)
    stage_bf16 = pltpu.VMEM((2, tb, tb), BF16)
    return pl.pallas_call(
        functools.partial(_sb_prompt_kernel, tb=tb, scale=dh ** -0.5),
        grid=(batch, nh),
        in_specs=[smem, smem, blk, blk, blk],
        out_specs=blk,
        out_shape=jax.ShapeDtypeStruct((nh, rows, dh), BF16),
        scratch_shapes=[stage_f32, stage_f32, stage_bf16, stage_bf16, stage_bf16,
                        pltpu.VMEM((tb, 1), F32), pltpu.VMEM((tb, dh), F32), stage_f32],
        compiler_params=_params(2),
        name="sb_attention_prompt",
    )(ij, bias, q, k, v)


def _sb_sample_kernel(pt_ref, q_ref, kn_ref, vn_ref, bias_ref, *refs, n_pg, scale):
    del pt_ref
    nt, nh, dh = kn_ref.shape
    ngrp = nh // SUBLANES
    k_refs = refs[:n_pg * ngrp]
    v_refs = refs[n_pg * ngrp:2 * n_pg * ngrp]
    o_ref, qf_ref, c_ref, acc_ref = refs[2 * n_pg * ngrp:]

    def head_rows(page_refs, p, h):
        flat = page_refs[p * ngrp + h // SUBLANES].reshape(PAGE_SIZE * SUBLANES, dh)
        return flat[pl.ds(h % SUBLANES, PAGE_SIZE, stride=SUBLANES), :]

    rpad = qf_ref.shape[0] // nh
    g = pl.program_id(1)
    bias_col = bias_ref[...]

    @pl.when(g == 0)
    def _():
        for h in range(nh):
            qf_ref[h * rpad:(h + 1) * rpad, :] = q_ref[:, h * dh:(h + 1) * dh]
        rr = lax.broadcasted_iota(jnp.int32, (nh * rpad, nh), 0)
        hh = lax.broadcasted_iota(jnp.int32, (nh * rpad, nh), 1)
        rep = jnp.where(rr // rpad == hh, 1.0, 0.0).astype(BF16)
        step = lax.broadcasted_iota(jnp.int32, (nh * rpad, 1), 0) % rpad
        qr = qf_ref[...].astype(BF16).astype(F32)
        zs, lss, valids = [], [], []
        for s in range(nt):
            kr = jnp.dot(rep, kn_ref[s].astype(BF16), preferred_element_type=F32)
            z = jnp.sum(qr * kr, axis=1, keepdims=True) * scale + bias_col
            valid = step > s
            zs.append(z)
            valids.append(valid)
            lss.append(jnp.where(valid, _neg_softplus(z), 0.0))
        acc = jnp.zeros(acc_ref.shape, F32)
        rc = jnp.zeros((nh * rpad, 1), F32)
        for s in reversed(range(nt)):
            rc = rc + lss[s]
            a = jnp.where(valids[s], jnp.exp(zs[s] + rc), 0.0)
            vr = jnp.dot(rep, vn_ref[s].astype(BF16), preferred_element_type=F32)
            acc = acc + a.astype(BF16).astype(F32) * vr
        acc_ref[...] = acc
        c_ref[...] = rc

    bf16_values = lambda x: x.astype(BF16).astype(F32)
    width = n_pg * PAGE_SIZE
    zrows = []
    for h in range(nh):
        qh = bf16_values(qf_ref[h * rpad:(h + 1) * rpad, :])
        kh = jnp.concatenate([head_rows(k_refs, p, h) for p in range(n_pg)], axis=0)
        zrows.append(lax.dot_general(qh, kh, (((1,), (1,)), ((), ())), preferred_element_type=F32))
    z = jnp.concatenate(zrows, axis=0) * scale + bias_col
    ls = _neg_softplus(z)
    tb = 2 * PAGE_SIZE
    tri = _lower_tri(tb)
    c = c_ref[...]
    a_blocks = [None] * (width // tb)
    for blk in reversed(range(width // tb)):
        rc = _suffix_sum(ls[:, blk * tb:(blk + 1) * tb], tri)
        a_blocks[blk] = bf16_values(jnp.exp(z[:, blk * tb:(blk + 1) * tb] + rc + c))
        c = c + rc[:, 0:1]
    c_ref[...] = c
    a = jnp.concatenate(a_blocks, axis=1)
    orows = []
    for h in range(nh):
        vh = jnp.concatenate([head_rows(v_refs, p, h) for p in range(n_pg)], axis=0)
        orows.append(jnp.dot(a[h * rpad:(h + 1) * rpad, :], vh, preferred_element_type=F32))
    acc_ref[...] += jnp.concatenate(orows, axis=0)

    @pl.when(g == pl.num_programs(1) - 1)
    def _():
        for h in range(nh):
            o_ref[:, h * dh:(h + 1) * dh] = acc_ref[h * rpad:(h + 1) * rpad, :]


def sb_attention_sample(q, k_new, v_new, bias, cache_k, cache_v, page_table, n_pg):
    nb, nt, d_attn = q.shape
    nh, dh = k_new.shape[2], k_new.shape[3]
    n_pages = page_table.shape[1]
    rpad = SUBLANES
    q_pad = jnp.pad(q, ((0, 0), (0, rpad - nt), (0, 0)))
    bias_col = jnp.repeat(bias.astype(F32), rpad).reshape(nh * rpad, 1)
    n_steps = n_pages // n_pg

    ngrp = nh // SUBLANES
    cache_k = cache_k.reshape(-1, PAGE_SIZE, ngrp, SUBLANES, dh)
    cache_v = cache_v.reshape(-1, PAGE_SIZE, ngrp, SUBLANES, dh)

    def page_spec(p, grp):
        return pl.BlockSpec(
            (None, PAGE_SIZE, None, SUBLANES, dh),
            lambda b, g, pt: (pt[b, n_pages - (g + 1) * n_pg + p], 0, grp, 0, 0))

    page_specs = [page_spec(p, grp) for p in range(n_pg) for grp in range(ngrp)]

    per_seq3 = lambda b, g, pt: (b, 0, 0)
    per_seq4 = lambda b, g, pt: (b, 0, 0, 0)
    grid_spec = pltpu.PrefetchScalarGridSpec(
        num_scalar_prefetch=1,
        grid=(nb, n_steps),
        in_specs=[pl.BlockSpec((None, rpad, d_attn), per_seq3),
                  pl.BlockSpec((None, nt, nh, dh), per_seq4),
                  pl.BlockSpec((None, nt, nh, dh), per_seq4),
                  pl.BlockSpec((nh * rpad, 1), lambda b, g, pt: (0, 0))]
                 + page_specs * 2,
        out_specs=pl.BlockSpec((None, rpad, d_attn), per_seq3),
        scratch_shapes=[pltpu.VMEM((nh * rpad, dh), F32), pltpu.VMEM((nh * rpad, 1), F32),
                        pltpu.VMEM((nh * rpad, dh), F32)],
    )
    return pl.pallas_call(
        functools.partial(_sb_sample_kernel, n_pg=n_pg, scale=dh ** -0.5),
        grid_spec=grid_spec,
        out_shape=jax.ShapeDtypeStruct((nb, rpad, d_attn), F32),
        compiler_params=_params(2),
        name="sb_attention_sample",
    )(page_table, q_pad, k_new, v_new, bias_col,
      *([cache_k] * len(page_specs)), *([cache_v] * len(page_specs)))


def kernel(x_prompt, x_sample, state_conv, state_pool, cache_k, cache_v, page_table, ln_g, ln_b, mix_w_in, conv_w, pool_w, pool_scale, mix_w_out, attn_w_qkv, attn_w_o, attn_bias, ffn_w_gate, ffn_w_up, ffn_w_down):
    bp, seq, d = x_prompt.shape
    bs, ts_, _ = x_sample.shape
    mp, ms = bp * seq, bs * ts_
    past_len = page_table.shape[1] * PAGE_SIZE
    xp = x_prompt.reshape(mp, d)
    xs = x_sample.reshape(ms, d)
    dp = pool_scale.shape[1]
    dc = conv_w.shape[2]
    d_attn = attn_w_o.shape[1]
    dh = d_attn // N_HEADS
    ln_g = ln_g.reshape(-1, 1, d)
    ln_b = ln_b.reshape(-1, 1, d)

    def ffn(xp, xpb, xs, xsb, layer):
        hp, hs, wd = ffn_up(xpb, xsb, ffn_w_gate, ffn_w_up, ffn_w_down, layer, 1024, 512, f"ffn_up_l{layer}")
        return matmul_residual_ln(hp, hs, wd, 0, xp, xs, ln_g, ln_b, 2 * layer + 1, 256,
                                  f"ffn_down_ln_l{layer}")

    wp = pool_w[0].astype(BF16)
    sp = pool_scale[0].reshape(1, dp)
    hp, hs = matmul(xp, xs, mix_w_in, 0, 0, mix_w_in.shape[2], F32, 1024, 1024, "mix_in")
    zp, conv_p, pool_p = mixer_prompt(hp, conv_w[0], wp, sp, bp, seq, 256)
    zs, conv_s, pool_s = mixer_sample(hs, state_conv[0], state_pool[0], conv_w[0], wp, sp, bs, ts_, past_len)
    xp, xpb, xs, xsb = matmul_residual_ln(zp, zs, mix_w_out, 0, xp, xs, ln_g, ln_b, 0, 512, "mix_out_ln")
    xp, xpb, xs, xsb = ffn(xp, xpb, xs, xsb, 0)

    bias = attn_bias[0]
    qh, qs = matmul_heads(xpb, xsb, attn_w_qkv, 0, 0, d_attn, dh, 1024, 1024, "q_proj", False)
    kp, kh, ks = matmul_heads(xpb, xsb, attn_w_qkv, 0, d_attn, d_attn, dh, 1024, 1024, "k_proj", True)
    vp, vh, vs = matmul_heads(xpb, xsb, attn_w_qkv, 0, 2 * d_attn, d_attn, dh, 1024, 1024, "v_proj", True)
    op = sb_attention_prompt(qh, kh, vh, bias, bp, seq, 256)
    ks4 = ks.reshape(bs, ts_, N_HEADS, dh)
    vs4 = vs.reshape(bs, ts_, N_HEADS, dh)
    o_pad = sb_attention_sample(qs.reshape(bs, ts_, d_attn), ks4, vs4, bias, cache_k[0], cache_v[0],
                                page_table, 8)
    os_ = o_pad[:, :ts_, :].reshape(ms, d_attn)
    xp, xpb, xs, xsb = matmul_residual_ln(op, os_, attn_w_o, 0, xp, xs, ln_g, ln_b, 2, 512, "attn_out_ln")
    xp, _, xs, _ = ffn(xp, xpb, xs, xsb, 1)

    return (xp.reshape(bp, seq, d), xs.reshape(bs, ts_, d),
            conv_p[None], conv_s[None], pool_p[None], pool_s[None],
            kp.reshape(1, bp, seq, N_HEADS, dh), vp.reshape(1, bp, seq, N_HEADS, dh),
            ks4[None], vs4[None])
```

```python
import functools

import jax
import jax.numpy as jnp
from jax import lax
from jax.experimental import pallas as pl
from jax.experimental.pallas import tpu as pltpu

F32 = jnp.float32
BF16 = jnp.bfloat16

CONV_WIDTH = 3
POOL_WINDOWS = (2, 4, 8, 16)
POOL_BUF = max(POOL_WINDOWS) - 1
N_HEADS = 16
PAGE_SIZE = 128
DEPTH = 2
ALPHA = (2.0 * DEPTH) ** 0.25
LN_EPS = 1e-5
MASKED_LOGIT = -1e30
STEPS_PER_TRIP = 40

VMEM_LIMIT_BYTES = 60 * 1024 * 1024
SUBLANES = 8


def _params(n_axes):
    return pltpu.CompilerParams(
        dimension_semantics=("arbitrary",) * n_axes,
        vmem_limit_bytes=VMEM_LIMIT_BYTES,
    )


def _dot(x, w):
    return jnp.dot(x.astype(BF16), w, preferred_element_type=F32)


def _mm_kernel(xm_ref, xe_ref, w_ref, om_ref, oe_ref, wb_ref):
    @pl.when(pl.program_id(1) == 0)
    def _():
        wb_ref[...] = w_ref[...].astype(BF16)
        oe_ref[...] = _dot(xe_ref[...], wb_ref[...]).astype(oe_ref.dtype)

    om_ref[...] = _dot(xm_ref[...], wb_ref[...]).astype(om_ref.dtype)


def matmul(xm, xe, w, layer, col0, n, out_dtype, tm, tn, name):
    m, k = xm.shape
    me = xe.shape[0]
    cb0 = col0 // tn
    return pl.pallas_call(
        _mm_kernel,
        grid=(n // tn, m // tm),
        in_specs=[pl.BlockSpec((tm, k), lambda j, i: (i, 0)),
                  pl.BlockSpec((me, k), lambda j, i: (0, 0)),
                  pl.BlockSpec((None, k, tn), lambda j, i: (layer, 0, cb0 + j))],
        out_specs=[pl.BlockSpec((tm, tn), lambda j, i: (i, j)),
                   pl.BlockSpec((me, tn), lambda j, i: (0, j))],
        out_shape=[jax.ShapeDtypeStruct((m, n), out_dtype), jax.ShapeDtypeStruct((me, n), out_dtype)],
        scratch_shapes=[pltpu.VMEM((k, tn), BF16)],
        compiler_params=_params(2),
        name=name,
    )(xm, xe, w)


def _mm_heads_kernel(xm_ref, xe_ref, w_ref, *refs, dh, row_major):
    om_ref = refs[0] if row_major else None
    oh_ref, oe_ref, wb_ref = refs[-3:]

    @pl.when(pl.program_id(1) == 0)
    def _():
        wb_ref[...] = w_ref[...].astype(BF16)
        oe_ref[...] = _dot(xe_ref[...], wb_ref[...])

    res = _dot(xm_ref[...], wb_ref[...])
    if row_major:
        om_ref[...] = res
    for h in range(oh_ref.shape[0]):
        oh_ref[h] = res[:, h * dh:(h + 1) * dh].astype(BF16)


def matmul_heads(xm, xe, w, layer, col0, n, dh, tm, tn, name, row_major):
    m, k = xm.shape
    me = xe.shape[0]
    cb0 = col0 // tn
    hpt = tn // dh
    out_specs = [pl.BlockSpec((hpt, tm, dh), lambda j, i: (j, i, 0)),
                 pl.BlockSpec((me, tn), lambda j, i: (0, j))]
    out_shape = [jax.ShapeDtypeStruct((n // dh, m, dh), BF16), jax.ShapeDtypeStruct((me, n), F32)]
    if row_major:
        out_specs.insert(0, pl.BlockSpec((tm, tn), lambda j, i: (i, j)))
        out_shape.insert(0, jax.ShapeDtypeStruct((m, n), F32))
    w_mode = {"pipeline_mode": pl.Buffered(1)} if n == tn else {}
    return pl.pallas_call(
        functools.partial(_mm_heads_kernel, dh=dh, row_major=row_major),
        grid=(n // tn, m // tm),
        in_specs=[pl.BlockSpec((tm, k), lambda j, i: (i, 0)),
                  pl.BlockSpec((me, k), lambda j, i: (0, 0)),
                  pl.BlockSpec((None, k, tn), lambda j, i: (layer, 0, cb0 + j), **w_mode)],
        out_specs=out_specs,
        out_shape=out_shape,
        scratch_shapes=[pltpu.VMEM((k, tn), BF16)],
        compiler_params=_params(2),
        name=name,
    )(xm, xe, w)


def _layer_norm_rows(s, g, b):
    mu = jnp.mean(s, axis=-1, keepdims=True)
    d = s - mu
    var = jnp.mean(d * d, axis=-1, keepdims=True)
    return d * lax.rsqrt(var + LN_EPS) * g + b


def _mm_ln_kernel(xm_ref, xe_ref, w_ref, rm_ref, re_ref, g_ref, b_ref,
                  ym_ref, ymb_ref, ye_ref, yeb_ref, *wb_refs):
    wb_ref = wb_refs[0] if wb_refs else w_ref

    def rows(x_ref, r_ref, y_ref, yb_ref):
        if len(x_ref.shape) == 3:
            x = jnp.concatenate([x_ref[h] for h in range(x_ref.shape[0])], axis=1)
        else:
            x = x_ref[...]
        y = _layer_norm_rows(ALPHA * r_ref[...] + _dot(x, wb_ref[...]), g_ref[...], b_ref[...])
        y_ref[...] = y
        yb_ref[...] = y.astype(BF16)

    @pl.when(pl.program_id(0) == 0)
    def _():
        if wb_refs:
            wb_ref[...] = w_ref[...].astype(BF16)
        rows(xe_ref, re_ref, ye_ref, yeb_ref)

    rows(xm_ref, rm_ref, ym_ref, ymb_ref)


def matmul_residual_ln(xm, xe, w, layer, rm, re, ln_g, ln_b, ln_row, tm, name):
    me, k = xe.shape
    m = xm.shape[-2]
    d = w.shape[-1]
    row = lambda i: (i, 0)
    const = lambda i: (0, 0)
    if xm.ndim == 3:
        xm_spec = pl.BlockSpec((xm.shape[0], tm, xm.shape[2]), lambda i: (0, i, 0))
    else:
        xm_spec = pl.BlockSpec((tm, k), row)
    if w.ndim == 2:
        w_spec = pl.BlockSpec((k, d), const, pipeline_mode=pl.Buffered(1))
        scratch = []
    else:
        w_spec = pl.BlockSpec((None, k, d), lambda i: (layer, 0, 0), pipeline_mode=pl.Buffered(1))
        scratch = [pltpu.VMEM((k, d), BF16)]
    ln_spec = pl.BlockSpec((None, 1, d), lambda i: (ln_row, 0, 0))
    return pl.pallas_call(
        _mm_ln_kernel,
        grid=(m // tm,),
        in_specs=[xm_spec, pl.BlockSpec((me, k), const), w_spec,
                  pl.BlockSpec((tm, d), row), pl.BlockSpec((me, d), const), ln_spec, ln_spec],
        out_specs=[pl.BlockSpec((tm, d), row), pl.BlockSpec((tm, d), row),
                   pl.BlockSpec((me, d), const), pl.BlockSpec((me, d), const)],
        out_shape=[jax.ShapeDtypeStruct((m, d), F32), jax.ShapeDtypeStruct((m, d), BF16),
                   jax.ShapeDtypeStruct((me, d), F32), jax.ShapeDtypeStruct((me, d), BF16)],
        scratch_shapes=scratch,
        compiler_params=_params(1),
        name=name,
    )(xm, xe, w, rm, re, ln_g, ln_b)


def _ffn_up_kernel(xm_ref, xe_ref, wg_ref, wu_ref, wd_ref, hm_ref, he_ref, wdb_ref, wgb_ref, wub_ref):
    def rows(x_ref, h_ref):
        x = x_ref[...]
        h_ref[...] = (jax.nn.silu(_dot(x, wgb_ref[...])) * _dot(x, wub_ref[...])).astype(h_ref.dtype)

    @pl.when(pl.program_id(1) == 0)
    def _():
        wgb_ref[...] = wg_ref[...].astype(BF16)
        wub_ref[...] = wu_ref[...].astype(BF16)
        rows(xe_ref, he_ref)

    wdb_ref[...] = wd_ref[...].astype(BF16)
    rows(xm_ref, hm_ref)


def ffn_up(xm, xe, wg, wu, wd, layer, tm, tf, name):
    m, k = xm.shape
    me = xe.shape[0]
    f = wg.shape[2]
    d = wd.shape[2]
    n_row_steps = m // tm
    slab = tf // n_row_steps
    col = lambda j, i: (layer, 0, j)
    return pl.pallas_call(
        _ffn_up_kernel,
        grid=(f // tf, m // tm),
        in_specs=[pl.BlockSpec((tm, k), lambda j, i: (i, 0)),
                  pl.BlockSpec((me, k), lambda j, i: (0, 0)),
                  pl.BlockSpec((None, k, tf), col), pl.BlockSpec((None, k, tf), col),
                  pl.BlockSpec((None, slab, d), lambda j, i: (layer, j * n_row_steps + i, 0))],
        out_specs=[pl.BlockSpec((tm, tf), lambda j, i: (i, j)),
                   pl.BlockSpec((me, tf), lambda j, i: (0, j)),
                   pl.BlockSpec((slab, d), lambda j, i: (j * n_row_steps + i, 0))],
        out_shape=[jax.ShapeDtypeStruct((m, f), BF16), jax.ShapeDtypeStruct((me, f), BF16),
                   jax.ShapeDtypeStruct((f, d), BF16)],
        scratch_shapes=[pltpu.VMEM((k, tf), BF16), pltpu.VMEM((k, tf), BF16)],
        compiler_params=_params(2),
        name=name,
    )(xm, xe, wg, wu, wd)


def _mixer_prompt_kernel(h_ref, wc_ref, wp_ref, sp_ref, z_ref, co_ref, po_ref, cbuf, pbuf, *, ts, dc):
    i = pl.program_id(1)
    ch, ph = SUBLANES, 2 * SUBLANES
    dp = pbuf.shape[1]
    pg = dp // len(POOL_WINDOWS)

    @pl.when(i == 0)
    def _():
        cbuf[0:ch, :] = jnp.zeros((ch, dc), F32)
        pbuf[0:ph, :] = jnp.zeros((ph, dp), F32)

    gb = h_ref[:, 0:dc]
    u = h_ref[:, dc:2 * dc] * h_ref[:, 2 * dc:3 * dc]
    pv = h_ref[:, 3 * dc:3 * dc + dp]
    cbuf[ch:ch + ts, :] = u
    pbuf[ph:ph + ts, :] = pv

    y = cbuf[ch - 2:ch - 2 + ts, :] * wc_ref[0:1, :]
    y = y + cbuf[ch - 1:ch - 1 + ts, :] * wc_ref[1:2, :]
    y = y + u * wc_ref[2:3, :]
    z_ref[:, 0:dc] = (gb * y).astype(z_ref.dtype)

    pos = i * ts + lax.broadcasted_iota(jnp.int32, (ts, 1), 0)
    for gi, w in enumerate(POOL_WINDOWS):
        lo = gi * pg
        cur = pv[:, lo:lo + pg]
        s = cur
        for back in range(1, w):
            s = s + pbuf[ph - back:ph - back + ts, lo:lo + pg]
        cnt = jnp.minimum(w, pos + 1).astype(F32)
        dlt = s / cnt - cur
        yg = jnp.dot(dlt.astype(BF16), wp_ref[gi], preferred_element_type=F32)
        z_ref[:, dc + lo:dc + lo + pg] = (yg * sp_ref[:, lo:lo + pg]).astype(z_ref.dtype)

    co_ref[...] = cbuf[ch + ts - (CONV_WIDTH - 1):ch + ts, :]
    po_ref[...] = pbuf[ph + ts - POOL_BUF:ph + ts, :]
    cbuf[0:ch, :] = cbuf[ts:ts + ch, :]
    pbuf[0:ph, :] = pbuf[ts:ts + ph, :]


def mixer_prompt(h, wc, wp, sp, batch, seq, ts):
    rows, width = h.shape
    dp = sp.shape[1]
    dc = (width - dp) // 3
    nt = seq // ts
    return pl.pallas_call(
        functools.partial(_mixer_prompt_kernel, ts=ts, dc=dc),
        grid=(batch, nt),
        in_specs=[pl.BlockSpec((ts, width), lambda b, i: (b * nt + i, 0)),
                  pl.BlockSpec(wc.shape, lambda b, i: (0, 0)),
                  pl.BlockSpec(wp.shape, lambda b, i: (0, 0, 0)),
                  pl.BlockSpec(sp.shape, lambda b, i: (0, 0))],
        out_specs=[pl.BlockSpec((ts, dc + dp), lambda b, i: (b * nt + i, 0)),
                   pl.BlockSpec((None, CONV_WIDTH - 1, dc), lambda b, i: (b, 0, 0)),
                   pl.BlockSpec((None, POOL_BUF, dp), lambda b, i: (b, 0, 0))],
        out_shape=[jax.ShapeDtypeStruct((rows, dc + dp), BF16),
                   jax.ShapeDtypeStruct((batch, CONV_WIDTH - 1, dc), F32),
                   jax.ShapeDtypeStruct((batch, POOL_BUF, dp), F32)],
        scratch_shapes=[pltpu.VMEM((ts + SUBLANES, dc), F32), pltpu.VMEM((ts + 2 * SUBLANES, dp), F32)],
        compiler_params=_params(2),
        name="mixer_prompt",
    )(h, wc, wp, sp)


def _mixer_sample_kernel(h_ref, sc_ref, spool_ref, wc_ref, wp_ref, sp_ref, z_ref, co_ref, po_ref,
                         *, nb, nt, dc, pos0):
    dp = sp_ref.shape[1]
    pg = dp // len(POOL_WINDOWS)
    nc = CONV_WIDTH - 1

    conv_ext = [sc_ref[j] for j in range(nc)]
    pool_ext = [spool_ref[j] for j in range(POOL_BUF)]
    gbs = []
    for t in range(nt):
        gbs.append(h_ref[t, :, 0:dc])
        conv_ext.append(h_ref[t, :, dc:2 * dc] * h_ref[t, :, 2 * dc:3 * dc])
        pool_ext.append(h_ref[t, :, 3 * dc:3 * dc + dp])

    for t in range(nt):
        y = conv_ext[t] * wc_ref[0:1, :]
        for kk in range(1, CONV_WIDTH):
            y = y + conv_ext[t + kk] * wc_ref[kk:kk + 1, :]
        z_ref[t, :, 0:dc] = gbs[t] * y
    for j in range(nc):
        co_ref[j] = conv_ext[nt + j]

    for gi, w in enumerate(POOL_WINDOWS):
        lo = gi * pg
        dls = []
        for t in range(nt):
            e = POOL_BUF + t
            s = pool_ext[e][:, lo:lo + pg]
            for back in range(1, w):
                s = s + pool_ext[e - back][:, lo:lo + pg]
            cnt = float(min(w, pos0 + t + 1))
            dls.append(s / cnt - pool_ext[e][:, lo:lo + pg])
        dl = jnp.concatenate(dls, axis=0).astype(BF16)
        yg = jnp.dot(dl, wp_ref[gi], preferred_element_type=F32) * sp_ref[:, lo:lo + pg]
        for t in range(nt):
            z_ref[t, :, dc + lo:dc + lo + pg] = yg[t * nb:(t + 1) * nb, :]
    for j in range(POOL_BUF):
        po_ref[j] = pool_ext[nt + j]


def mixer_sample(h, state_conv, state_pool, wc, wp, sp, nb, nt, pos0):
    rows, width = h.shape
    dp = sp.shape[1]
    dc = (width - dp) // 3
    nc = CONV_WIDTH - 1
    step_major = lambda a: jnp.swapaxes(a, 0, 1)
    z, conv_new, pool_new = pl.pallas_call(
        functools.partial(_mixer_sample_kernel, nb=nb, nt=nt, dc=dc, pos0=pos0),
        out_shape=[jax.ShapeDtypeStruct((nt, nb, dc + dp), F32),
                   jax.ShapeDtypeStruct((nc, nb, dc), F32),
                   jax.ShapeDtypeStruct((POOL_BUF, nb, dp), F32)],
        compiler_params=pltpu.CompilerParams(vmem_limit_bytes=VMEM_LIMIT_BYTES),
        name="mixer_sample",
    )(step_major(h.reshape(nb, nt, width)), step_major(state_conv), step_major(state_pool), wc, wp, sp)
    return step_major(z).reshape(rows, dc + dp), step_major(conv_new), step_major(pool_new)


def _neg_softplus(z):
    return -(jnp.maximum(z, 0.0) + jnp.log1p(jnp.exp(-jnp.abs(z))))


def _suffix_sum(ls, tri):
    hi = ls.astype(BF16)
    lo = (ls - hi.astype(F32)).astype(BF16)
    return (jnp.dot(hi, tri, preferred_element_type=F32)
            + jnp.dot(lo, tri, preferred_element_type=F32))


def _lower_tri(n):
    r = lax.broadcasted_iota(jnp.int32, (n, n), 0)
    c = lax.broadcasted_iota(jnp.int32, (n, n), 1)
    return jnp.where(r >= c, 1.0, 0.0).astype(BF16)


def _sb_prompt_kernel(ij_ref, bias_ref, qb_ref, kb_ref, vb_ref, o_ref,
                      zr_ref, z_ref, hi_ref, lo_ref, e_ref, c_ref, acc_ref, offset_ref, *, tb, scale):
    npairs = ij_ref.shape[1]
    bias = bias_ref[pl.program_id(1)]
    for ref in (zr_ref, z_ref, hi_ref, lo_ref, e_ref, c_ref, acc_ref):
        ref[...] = jnp.zeros(ref.shape, ref.dtype)
    tri = _lower_tri(tb)
    hidden = jnp.where(lax.broadcasted_iota(jnp.int32, (tb, tb), 1)
                       < lax.broadcasted_iota(jnp.int32, (tb, tb), 0), 0.0, MASKED_LOGIT)
    offset_ref[0] = jnp.full((tb, tb), bias, F32)
    offset_ref[1] = bias + hidden

    def pair(p):
        p = jnp.clip(p, 0, npairs - 1)
        return ij_ref[0, p], ij_ref[1, p]

    def block_rows(ref, blk):
        return ref[pl.ds(pl.multiple_of(blk * tb, tb), tb), :]

    def step(n, s):
        t = 1 - s
        i, j = pair(n)
        zr_ref[s] = lax.dot_general(block_rows(qb_ref, i), block_rows(kb_ref, j),
                                    (((1,), (1,)), ((), ())), preferred_element_type=F32)

        i, j = pair(n - 1)
        z = zr_ref[t] * scale + offset_ref[(i == j).astype(jnp.int32)]
        nl = jnp.maximum(z, 0.0) + jnp.log(1.0 + jnp.exp(-jnp.abs(z)))
        hi = nl.astype(BF16)
        z_ref[t] = z
        hi_ref[t] = hi
        lo_ref[t] = (nl - hi.astype(F32)).astype(BF16)

        i, j = pair(n - 2)
        rc = (jnp.dot(hi_ref[s], tri, preferred_element_type=F32)
              + jnp.dot(lo_ref[s], tri, preferred_element_type=F32))
        c = jnp.where(i == j, 0.0, c_ref[...])
        e_ref[s] = jnp.exp(z_ref[s] - rc - c).astype(BF16)
        c_ref[...] = c + rc[:, 0:1]

        i, j = pair(n - 3)
        acc = jnp.where(i == j, 0.0, acc_ref[...])
        new = acc + jnp.dot(e_ref[t], block_rows(vb_ref, j), preferred_element_type=F32)
        acc = jnp.where(n - 3 < npairs, new, acc)
        acc_ref[...] = acc
        o_ref[pl.ds(pl.multiple_of(i * tb, tb), tb), :] = acc.astype(o_ref.dtype)

    def trip(m, carry):
        for r in range(STEPS_PER_TRIP):
            step(STEPS_PER_TRIP * m + r, r % 2)
        return carry

    n_steps = npairs + 3
    lax.fori_loop(0, -(-n_steps // STEPS_PER_TRIP), trip, 0)


def sb_attention_prompt(q, k, v, bias, batch, seq, tb):
    nh, rows, dh = q.shape
    nblk = seq // tb
    pairs = [(i, j) for i in range(nblk) for j in range(i, -1, -1)]
    ij = jnp.asarray(pairs, dtype=jnp.int32).T
    blk = pl.BlockSpec((None, seq, dh), lambda b, h: (h, b, 0))
    smem = pl.BlockSpec(memory_space=pltpu.SMEM)
    stage_f32 = pltpu.VMEM((2, tb, tb), F32)
    stage_bf16 = pltpu.VMEM((2, tb, tb), BF16)
    return pl.pallas_call(
        functools.partial(_sb_prompt_kernel, tb=tb, scale=dh ** -0.5),
        grid=(batch, nh),
        in_specs=[smem, smem, blk, blk, blk],
        out_specs=blk,
        out_shape=jax.ShapeDtypeStruct((nh, rows, dh), BF16),
        scratch_shapes=[stage_f32, stage_f32, stage_bf16, stage_bf16, stage_bf16,
                        pltpu.VMEM((tb, 1), F32), pltpu.VMEM((tb, dh), F32), stage_f32],
        compiler_params=_params(2),
        name="sb_attention_prompt",
    )(ij, bias, q, k, v)


def _sb_sample_kernel(pt_ref, q_ref, kn_ref, vn_ref, bias_ref, *refs, n_pg, scale):
    del pt_ref
    nt, nh, dh = kn_ref.shape
    ngrp = nh // SUBLANES
    k_refs = refs[:n_pg * ngrp]
    v_refs = refs[n_pg * ngrp:2 * n_pg * ngrp]
    o_ref, qf_ref, c_ref, acc_ref = refs[2 * n_pg * ngrp:]

    def head_rows(page_refs, p, h):
        flat = page_refs[p * ngrp + h // SUBLANES].reshape(PAGE_SIZE * SUBLANES, dh)
        return flat[pl.ds(h % SUBLANES, PAGE_SIZE, stride=SUBLANES), :]

    rpad = qf_ref.shape[0] // nh
    g = pl.program_id(1)
    bias_col = bias_ref[...]

    @pl.when(g == 0)
    def _():
        for h in range(nh):
            qf_ref[h * rpad:(h + 1) * rpad, :] = q_ref[:, h * dh:(h + 1) * dh]
        rr = lax.broadcasted_iota(jnp.int32, (nh * rpad, nh), 0)
        hh = lax.broadcasted_iota(jnp.int32, (nh * rpad, nh), 1)
        rep = jnp.where(rr // rpad == hh, 1.0, 0.0).astype(BF16)
        step = lax.broadcasted_iota(jnp.int32, (nh * rpad, 1), 0) % rpad
        qr = qf_ref[...].astype(BF16).astype(F32)
        zs, lss, valids = [], [], []
        for s in range(nt):
            kr = jnp.dot(rep, kn_ref[s].astype(BF16), preferred_element_type=F32)
            z = jnp.sum(qr * kr, axis=1, keepdims=True) * scale + bias_col
            valid = step > s
            zs.append(z)
            valids.append(valid)
            lss.append(jnp.where(valid, _neg_softplus(z), 0.0))
        acc = jnp.zeros(acc_ref.shape, F32)
        rc = jnp.zeros((nh * rpad, 1), F32)
        for s in reversed(range(nt)):
            rc = rc + lss[s]
            a = jnp.where(valids[s], jnp.exp(zs[s] + rc), 0.0)
            vr = jnp.dot(rep, vn_ref[s].astype(BF16), preferred_element_type=F32)
            acc = acc + a.astype(BF16).astype(F32) * vr
        acc_ref[...] = acc
        c_ref[...] = rc

    bf16_values = lambda x: x.astype(BF16).astype(F32)
    width = n_pg * PAGE_SIZE
    zrows = []
    for h in range(nh):
        qh = bf16_values(qf_ref[h * rpad:(h + 1) * rpad, :])
        kh = jnp.concatenate([head_rows(k_refs, p, h) for p in range(n_pg)], axis=0)
        zrows.append(lax.dot_general(qh, kh, (((1,), (1,)), ((), ())), preferred_element_type=F32))
    z = jnp.concatenate(zrows, axis=0) * scale + bias_col
    ls = _neg_softplus(z)
    tb = 2 * PAGE_SIZE
    tri = _lower_tri(tb)
    c = c_ref[...]
    a_blocks = [None] * (width // tb)
    for blk in reversed(range(width // tb)):
        rc = _suffix_sum(ls[:, blk * tb:(blk + 1) * tb], tri)
        a_blocks[blk] = bf16_values(jnp.exp(z[:, blk * tb:(blk + 1) * tb] + rc + c))
        c = c + rc[:, 0:1]
    c_ref[...] = c
    a = jnp.concatenate(a_blocks, axis=1)
    orows = []
    for h in range(nh):
        vh = jnp.concatenate([head_rows(v_refs, p, h) for p in range(n_pg)], axis=0)
        orows.append(jnp.dot(a[h * rpad:(h + 1) * rpad, :], vh, preferred_element_type=F32))
    acc_ref[...] += jnp.concatenate(orows, axis=0)

    @pl.when(g == pl.num_programs(1) - 1)
    def _():
        for h in range(nh):
            o_ref[:, h * dh:(h + 1) * dh] = acc_ref[h * rpad:(h + 1) * rpad, :]


def sb_attention_sample(q, k_new, v_new, bias, cache_k, cache_v, page_table, n_pg):
    nb, nt, d_attn = q.shape
    nh, dh = k_new.shape[2], k_new.shape[3]
    n_pages = page_table.shape[1]
    rpad = SUBLANES
    q_pad = jnp.pad(q, ((0, 0), (0, rpad - nt), (0, 0)))
    bias_col = jnp.repeat(bias.astype(F32), rpad).reshape(nh * rpad, 1)
    n_steps = n_pages // n_pg

    ngrp = nh // SUBLANES
    cache_k = cache_k.reshape(-1, PAGE_SIZE, ngrp, SUBLANES, dh)
    cache_v = cache_v.reshape(-1, PAGE_SIZE, ngrp, SUBLANES, dh)

    def page_spec(p, grp):
        return pl.BlockSpec(
            (None, PAGE_SIZE, None, SUBLANES, dh),
            lambda b, g, pt: (pt[b, n_pages - (g + 1) * n_pg + p], 0, grp, 0, 0))

    page_specs = [page_spec(p, grp) for p in range(n_pg) for grp in range(ngrp)]

    per_seq3 = lambda b, g, pt: (b, 0, 0)
    per_seq4 = lambda b, g, pt: (b, 0, 0, 0)
    grid_spec = pltpu.PrefetchScalarGridSpec(
        num_scalar_prefetch=1,
        grid=(nb, n_steps),
        in_specs=[pl.BlockSpec((None, rpad, d_attn), per_seq3),
                  pl.BlockSpec((None, nt, nh, dh), per_seq4),
                  pl.BlockSpec((None, nt, nh, dh), per_seq4),
                  pl.BlockSpec((nh * rpad, 1), lambda b, g, pt: (0, 0))]
                 + page_specs * 2,
        out_specs=pl.BlockSpec((None, rpad, d_attn), per_seq3),
        scratch_shapes=[pltpu.VMEM((nh * rpad, dh), F32), pltpu.VMEM((nh * rpad, 1), F32),
                        pltpu.VMEM((nh * rpad, dh), F32)],
    )
    return pl.pallas_call(
        functools.partial(_sb_sample_kernel, n_pg=n_pg, scale=dh ** -0.5),
        grid_spec=grid_spec,
        out_shape=jax.ShapeDtypeStruct((nb, rpad, d_attn), F32),
        compiler_params=_params(2),
        name="sb_attention_sample",
    )(page_table, q_pad, k_new, v_new, bias_col,
      *([cache_k] * len(page_specs)), *([cache_v] * len(page_specs)))


def kernel(x_prompt, x_sample, state_conv, state_pool, cache_k, cache_v, page_table, ln_g, ln_b, mix_w_in, conv_w, pool_w, pool_scale, mix_w_out, attn_w_qkv, attn_w_o, attn_bias, ffn_w_gate, ffn_w_up, ffn_w_down):
    bp, seq, d = x_prompt.shape
    bs, ts_, _ = x_sample.shape
    mp, ms = bp * seq, bs * ts_
    past_len = page_table.shape[1] * PAGE_SIZE
    xp = x_prompt.reshape(mp, d)
    xs = x_sample.reshape(ms, d)
    dp = pool_scale.shape[1]
    dc = conv_w.shape[2]
    d_attn = attn_w_o.shape[1]
    dh = d_attn // N_HEADS
    ln_g = ln_g.reshape(-1, 1, d)
    ln_b = ln_b.reshape(-1, 1, d)

    def ffn(xp, xpb, xs, xsb, layer):
        hp, hs, wd = ffn_up(xpb, xsb, ffn_w_gate, ffn_w_up, ffn_w_down, layer, 1024, 512, f"ffn_up_l{layer}")
        return matmul_residual_ln(hp, hs, wd, 0, xp, xs, ln_g, ln_b, 2 * layer + 1, 256,
                                  f"ffn_down_ln_l{layer}")

    wp = pool_w[0].astype(BF16)
    sp = pool_scale[0].reshape(1, dp)
    hp, hs = matmul(xp, xs, mix_w_in, 0, 0, mix_w_in.shape[2], F32, 1024, 1024, "mix_in")
    zp, conv_p, pool_p = mixer_prompt(hp, conv_w[0], wp, sp, bp, seq, 512)
    zs, conv_s, pool_s = mixer_sample(hs, state_conv[0], state_pool[0], conv_w[0], wp, sp, bs, ts_, past_len)
    xp, xpb, xs, xsb = matmul_residual_ln(zp, zs, mix_w_out, 0, xp, xs, ln_g, ln_b, 0, 512, "mix_out_ln")
    xp, xpb, xs, xsb = ffn(xp, xpb, xs, xsb, 0)

    bias = attn_bias[0]
    qh, qs = matmul_heads(xpb, xsb, attn_w_qkv, 0, 0, d_attn, dh, 1024, d_attn, "q_proj", False)
    kp, kh, ks = matmul_heads(xpb, xsb, attn_w_qkv, 0, d_attn, d_attn, dh, 512, d_attn, "k_proj", True)
    vp, vh, vs = matmul_heads(xpb, xsb, attn_w_qkv, 0, 2 * d_attn, d_attn, dh, 1024, 1024, "v_proj", True)
    op = sb_attention_prompt(qh, kh, vh, bias, bp, seq, 256)
    ks4 = ks.reshape(bs, ts_, N_HEADS, dh)
    vs4 = vs.reshape(bs, ts_, N_HEADS, dh)
    o_pad = sb_attention_sample(qs.reshape(bs, ts_, d_attn), ks4, vs4, bias, cache_k[0], cache_v[0],
                                page_table, 8)
    os_ = o_pad[:, :ts_, :].reshape(ms, d_attn)
    xp, xpb, xs, xsb = matmul_residual_ln(op, os_, attn_w_o, 0, xp, xs, ln_g, ln_b, 2, 512, "attn_out_ln")
    xp, _, xs, _ = ffn(xp, xpb, xs, xsb, 1)

    return (xp.reshape(bp, seq, d), xs.reshape(bs, ts_, d),
            conv_p[None], conv_s[None], pool_p[None], pool_s[None],
            kp.reshape(1, bp, seq, N_HEADS, dh), vp.reshape(1, bp, seq, N_HEADS, dh),
            ks4[None], vs4[None])
```

```python
import functools

import jax
import jax.numpy as jnp
from jax import lax
from jax.experimental import pallas as pl
from jax.experimental.pallas import tpu as pltpu

F32 = jnp.float32
BF16 = jnp.bfloat16

CONV_WIDTH = 3
POOL_WINDOWS = (2, 4, 8, 16)
POOL_BUF = max(POOL_WINDOWS) - 1
N_HEADS = 16
PAGE_SIZE = 128
DEPTH = 2
ALPHA = (2.0 * DEPTH) ** 0.25
LN_EPS = 1e-5
MASKED_LOGIT = -1e30
STEPS_PER_TRIP = 40

VMEM_LIMIT_BYTES = 60 * 1024 * 1024
SUBLANES = 8


def _params(n_axes):
    return pltpu.CompilerParams(
        dimension_semantics=("arbitrary",) * n_axes,
        vmem_limit_bytes=VMEM_LIMIT_BYTES,
    )


def _dot(x, w):
    return jnp.dot(x.astype(BF16), w, preferred_element_type=F32)


def _mm_kernel(xm_ref, xe_ref, w_ref, om_ref, oe_ref, wb_ref):
    @pl.when(pl.program_id(1) == 0)
    def _():
        wb_ref[...] = w_ref[...].astype(BF16)
        oe_ref[...] = _dot(xe_ref[...], wb_ref[...]).astype(oe_ref.dtype)

    om_ref[...] = _dot(xm_ref[...], wb_ref[...]).astype(om_ref.dtype)


def matmul(xm, xe, w, layer, col0, n, out_dtype, tm, tn, name):
    m, k = xm.shape
    me = xe.shape[0]
    cb0 = col0 // tn
    return pl.pallas_call(
        _mm_kernel,
        grid=(n // tn, m // tm),
        in_specs=[pl.BlockSpec((tm, k), lambda j, i: (i, 0)),
                  pl.BlockSpec((me, k), lambda j, i: (0, 0)),
                  pl.BlockSpec((None, k, tn), lambda j, i: (layer, 0, cb0 + j))],
        out_specs=[pl.BlockSpec((tm, tn), lambda j, i: (i, j)),
                   pl.BlockSpec((me, tn), lambda j, i: (0, j))],
        out_shape=[jax.ShapeDtypeStruct((m, n), out_dtype), jax.ShapeDtypeStruct((me, n), out_dtype)],
        scratch_shapes=[pltpu.VMEM((k, tn), BF16)],
        compiler_params=_params(2),
        name=name,
    )(xm, xe, w)


def _mm_heads_kernel(xm_ref, xe_ref, w_ref, *refs, dh, row_major):
    om_ref = refs[0] if row_major else None
    oh_ref, oe_ref, wb_ref = refs[-3:]

    @pl.when(pl.program_id(1) == 0)
    def _():
        wb_ref[...] = w_ref[...].astype(BF16)
        oe_ref[...] = _dot(xe_ref[...], wb_ref[...])

    res = _dot(xm_ref[...], wb_ref[...])
    if row_major:
        om_ref[...] = res
    for h in range(oh_ref.shape[0]):
        oh_ref[h] = res[:, h * dh:(h + 1) * dh].astype(BF16)


def matmul_heads(xm, xe, w, layer, col0, n, dh, tm, tn, name, row_major):
    m, k = xm.shape
    me = xe.shape[0]
    cb0 = col0 // tn
    hpt = tn // dh
    out_specs = [pl.BlockSpec((hpt, tm, dh), lambda j, i: (j, i, 0)),
                 pl.BlockSpec((me, tn), lambda j, i: (0, j))]
    out_shape = [jax.ShapeDtypeStruct((n // dh, m, dh), BF16), jax.ShapeDtypeStruct((me, n), F32)]
    if row_major:
        out_specs.insert(0, pl.BlockSpec((tm, tn), lambda j, i: (i, j)))
        out_shape.insert(0, jax.ShapeDtypeStruct((m, n), F32))
    w_mode = {"pipeline_mode": pl.Buffered(1)} if n == tn else {}
    return pl.pallas_call(
        functools.partial(_mm_heads_kernel, dh=dh, row_major=row_major),
        grid=(n // tn, m // tm),
        in_specs=[pl.BlockSpec((tm, k), lambda j, i: (i, 0)),
                  pl.BlockSpec((me, k), lambda j, i: (0, 0)),
                  pl.BlockSpec((None, k, tn), lambda j, i: (layer, 0, cb0 + j), **w_mode)],
        out_specs=out_specs,
        out_shape=out_shape,
        scratch_shapes=[pltpu.VMEM((k, tn), BF16)],
        compiler_params=_params(2),
        name=name,
    )(xm, xe, w)


def _layer_norm_rows(s, g, b):
    mu = jnp.mean(s, axis=-1, keepdims=True)
    d = s - mu
    var = jnp.mean(d * d, axis=-1, keepdims=True)
    return d * lax.rsqrt(var + LN_EPS) * g + b


def _mm_ln_kernel(xm_ref, xe_ref, w_ref, rm_ref, re_ref, g_ref, b_ref,
                  ym_ref, ymb_ref, ye_ref, yeb_ref, *wb_refs):
    wb_ref = wb_refs[0] if wb_refs else w_ref

    def rows(x_ref, r_ref, y_ref, yb_ref):
        if len(x_ref.shape) == 3:
            x = jnp.concatenate([x_ref[h] for h in range(x_ref.shape[0])], axis=1)
        else:
            x = x_ref[...]
        y = _layer_norm_rows(ALPHA * r_ref[...] + _dot(x, wb_ref[...]), g_ref[...], b_ref[...])
        y_ref[...] = y
        yb_ref[...] = y.astype(BF16)

    @pl.when(pl.program_id(0) == 0)
    def _():
        if wb_refs:
            wb_ref[...] = w_ref[...].astype(BF16)
        rows(xe_ref, re_ref, ye_ref, yeb_ref)

    rows(xm_ref, rm_ref, ym_ref, ymb_ref)


def matmul_residual_ln(xm, xe, w, layer, rm, re, ln_g, ln_b, ln_row, tm, name):
    me, k = xe.shape
    m = xm.shape[-2]
    d = w.shape[-1]
    row = lambda i: (i, 0)
    const = lambda i: (0, 0)
    if xm.ndim == 3:
        xm_spec = pl.BlockSpec((xm.shape[0], tm, xm.shape[2]), lambda i: (0, i, 0))
    else:
        xm_spec = pl.BlockSpec((tm, k), row)
    if w.ndim == 2:
        w_spec = pl.BlockSpec((k, d), const, pipeline_mode=pl.Buffered(1))
        scratch = []
    else:
        w_spec = pl.BlockSpec((None, k, d), lambda i: (layer, 0, 0), pipeline_mode=pl.Buffered(1))
        scratch = [pltpu.VMEM((k, d), BF16)]
    ln_spec = pl.BlockSpec((None, 1, d), lambda i: (ln_row, 0, 0))
    return pl.pallas_call(
        _mm_ln_kernel,
        grid=(m // tm,),
        in_specs=[xm_spec, pl.BlockSpec((me, k), const), w_spec,
                  pl.BlockSpec((tm, d), row), pl.BlockSpec((me, d), const), ln_spec, ln_spec],
        out_specs=[pl.BlockSpec((tm, d), row), pl.BlockSpec((tm, d), row),
                   pl.BlockSpec((me, d), const), pl.BlockSpec((me, d), const)],
        out_shape=[jax.ShapeDtypeStruct((m, d), F32), jax.ShapeDtypeStruct((m, d), BF16),
                   jax.ShapeDtypeStruct((me, d), F32), jax.ShapeDtypeStruct((me, d), BF16)],
        scratch_shapes=scratch,
        compiler_params=_params(1),
        name=name,
    )(xm, xe, w, rm, re, ln_g, ln_b)


def _ffn_up_kernel(xm_ref, xe_ref, wg_ref, wu_ref, wd_ref, hm_ref, he_ref, wdb_ref, wgb_ref, wub_ref):
    j = pl.program_id(0)
    i = pl.program_id(1)
    slab = wg_ref.shape[0]
    cur = (j + 1) % 2
    nxt = j % 2

    def rows(x_ref, h_ref):
        x = x_ref[...]
        h_ref[...] = (jax.nn.silu(_dot(x, wgb_ref[cur])) * _dot(x, wub_ref[cur])).astype(h_ref.dtype)

    @pl.when(jnp.logical_and(j > 0, i == 0))
    def _():
        rows(xe_ref, he_ref)

    @pl.when(j > 0)
    def _():
        rows(xm_ref, hm_ref)
        wdb_ref[...] = wd_ref[...].astype(BF16)

    at = pl.ds(pl.multiple_of(i * slab, slab), slab)
    wgb_ref[nxt, at, :] = wg_ref[...].astype(BF16)
    wub_ref[nxt, at, :] = wu_ref[...].astype(BF16)


def ffn_up(xm, xe, wg, wu, wd, layer, tm, tf, name):
    m, k = xm.shape
    me = xe.shape[0]
    f = wg.shape[2]
    d = wd.shape[2]
    n_cols = f // tf
    n_rows = m // tm
    w_slab = k // n_rows
    wd_slab = tf // n_rows
    tile = lambda j: jnp.maximum(j - 1, 0)
    row = lambda j, i: jnp.where(j == 0, 0, i)
    wd_row = lambda j, i: jnp.where(j == 0, 0, (j - 1) * n_rows + i)
    w_spec = pl.BlockSpec((None, w_slab, tf), lambda j, i: (layer, i, jnp.minimum(j, n_cols - 1)))
    return pl.pallas_call(
        _ffn_up_kernel,
        grid=(n_cols + 1, n_rows),
        in_specs=[pl.BlockSpec((tm, k), lambda j, i: (row(j, i), 0)),
                  pl.BlockSpec((me, k), lambda j, i: (0, 0)),
                  w_spec, w_spec,
                  pl.BlockSpec((None, wd_slab, d), lambda j, i: (layer, wd_row(j, i), 0))],
        out_specs=[pl.BlockSpec((tm, tf), lambda j, i: (row(j, i), tile(j))),
                   pl.BlockSpec((me, tf), lambda j, i: (0, tile(j))),
                   pl.BlockSpec((wd_slab, d), lambda j, i: (wd_row(j, i), 0))],
        out_shape=[jax.ShapeDtypeStruct((m, f), BF16), jax.ShapeDtypeStruct((me, f), BF16),
                   jax.ShapeDtypeStruct((f, d), BF16)],
        scratch_shapes=[pltpu.VMEM((2, k, tf), BF16), pltpu.VMEM((2, k, tf), BF16)],
        compiler_params=_params(2),
        name=name,
    )(xm, xe, wg, wu, wd)


def _mixer_prompt_kernel(h_ref, wc_ref, wp_ref, sp_ref, z_ref, co_ref, po_ref, cbuf, pbuf, *, ts, dc):
    i = pl.program_id(1)
    ch, ph = SUBLANES, 2 * SUBLANES
    dp = pbuf.shape[1]
    pg = dp // len(POOL_WINDOWS)

    @pl.when(i == 0)
    def _():
        cbuf[0:ch, :] = jnp.zeros((ch, dc), F32)
        pbuf[0:ph, :] = jnp.zeros((ph, dp), F32)

    gb = h_ref[:, 0:dc]
    u = h_ref[:, dc:2 * dc] * h_ref[:, 2 * dc:3 * dc]
    pv = h_ref[:, 3 * dc:3 * dc + dp]
    cbuf[ch:ch + ts, :] = u
    pbuf[ph:ph + ts, :] = pv

    y = cbuf[ch - 2:ch - 2 + ts, :] * wc_ref[0:1, :]
    y = y + cbuf[ch - 1:ch - 1 + ts, :] * wc_ref[1:2, :]
    y = y + u * wc_ref[2:3, :]
    z_ref[:, 0:dc] = (gb * y).astype(z_ref.dtype)

    pos = i * ts + lax.broadcasted_iota(jnp.int32, (ts, 1), 0)
    for gi, w in enumerate(POOL_WINDOWS):
        lo = gi * pg
        cur = pv[:, lo:lo + pg]
        s = cur
        for back in range(1, w):
            s = s + pbuf[ph - back:ph - back + ts, lo:lo + pg]
        cnt = jnp.minimum(w, pos + 1).astype(F32)
        dlt = s / cnt - cur
        yg = jnp.dot(dlt.astype(BF16), wp_ref[gi], preferred_element_type=F32)
        z_ref[:, dc + lo:dc + lo + pg] = (yg * sp_ref[:, lo:lo + pg]).astype(z_ref.dtype)

    co_ref[...] = cbuf[ch + ts - (CONV_WIDTH - 1):ch + ts, :]
    po_ref[...] = pbuf[ph + ts - POOL_BUF:ph + ts, :]
    cbuf[0:ch, :] = cbuf[ts:ts + ch, :]
    pbuf[0:ph, :] = pbuf[ts:ts + ph, :]


def mixer_prompt(h, wc, wp, sp, batch, seq, ts):
    rows, width = h.shape
    dp = sp.shape[1]
    dc = (width - dp) // 3
    nt = seq // ts
    return pl.pallas_call(
        functools.partial(_mixer_prompt_kernel, ts=ts, dc=dc),
        grid=(batch, nt),
        in_specs=[pl.BlockSpec((ts, width), lambda b, i: (b * nt + i, 0)),
                  pl.BlockSpec(wc.shape, lambda b, i: (0, 0)),
                  pl.BlockSpec(wp.shape, lambda b, i: (0, 0, 0)),
                  pl.BlockSpec(sp.shape, lambda b, i: (0, 0))],
        out_specs=[pl.BlockSpec((ts, dc + dp), lambda b, i: (b * nt + i, 0)),
                   pl.BlockSpec((None, CONV_WIDTH - 1, dc), lambda b, i: (b, 0, 0)),
                   pl.BlockSpec((None, POOL_BUF, dp), lambda b, i: (b, 0, 0))],
        out_shape=[jax.ShapeDtypeStruct((rows, dc + dp), BF16),
                   jax.ShapeDtypeStruct((batch, CONV_WIDTH - 1, dc), F32),
                   jax.ShapeDtypeStruct((batch, POOL_BUF, dp), F32)],
        scratch_shapes=[pltpu.VMEM((ts + SUBLANES, dc), F32), pltpu.VMEM((ts + 2 * SUBLANES, dp), F32)],
        compiler_params=_params(2),
        name="mixer_prompt",
    )(h, wc, wp, sp)


def _mixer_sample_kernel(h_ref, sc_ref, spool_ref, wc_ref, wp_ref, sp_ref, z_ref, co_ref, po_ref,
                         *, nb, nt, dc, pos0):
    dp = sp_ref.shape[1]
    pg = dp // len(POOL_WINDOWS)
    nc = CONV_WIDTH - 1

    conv_ext = [sc_ref[j] for j in range(nc)]
    pool_ext = [spool_ref[j] for j in range(POOL_BUF)]
    gbs = []
    for t in range(nt):
        gbs.append(h_ref[t, :, 0:dc])
        conv_ext.append(h_ref[t, :, dc:2 * dc] * h_ref[t, :, 2 * dc:3 * dc])
        pool_ext.append(h_ref[t, :, 3 * dc:3 * dc + dp])

    for t in range(nt):
        y = conv_ext[t] * wc_ref[0:1, :]
        for kk in range(1, CONV_WIDTH):
            y = y + conv_ext[t + kk] * wc_ref[kk:kk + 1, :]
        z_ref[t, :, 0:dc] = gbs[t] * y
    for j in range(nc):
        co_ref[j] = conv_ext[nt + j]

    for gi, w in enumerate(POOL_WINDOWS):
        lo = gi * pg
        dls = []
        for t in range(nt):
            e = POOL_BUF + t
            s = pool_ext[e][:, lo:lo + pg]
            for back in range(1, w):
                s = s + pool_ext[e - back][:, lo:lo + pg]
            cnt = float(min(w, pos0 + t + 1))
            dls.append(s / cnt - pool_ext[e][:, lo:lo + pg])
        dl = jnp.concatenate(dls, axis=0).astype(BF16)
        yg = jnp.dot(dl, wp_ref[gi], preferred_element_type=F32) * sp_ref[:, lo:lo + pg]
        for t in range(nt):
            z_ref[t, :, dc + lo:dc + lo + pg] = yg[t * nb:(t + 1) * nb, :]
    for j in range(POOL_BUF):
        po_ref[j] = pool_ext[nt + j]


def mixer_sample(h, state_conv, state_pool, wc, wp, sp, nb, nt, pos0):
    rows, width = h.shape
    dp = sp.shape[1]
    dc = (width - dp) // 3
    nc = CONV_WIDTH - 1
    step_major = lambda a: jnp.swapaxes(a, 0, 1)
    z, conv_new, pool_new = pl.pallas_call(
        functools.partial(_mixer_sample_kernel, nb=nb, nt=nt, dc=dc, pos0=pos0),
        out_shape=[jax.ShapeDtypeStruct((nt, nb, dc + dp), F32),
                   jax.ShapeDtypeStruct((nc, nb, dc), F32),
                   jax.ShapeDtypeStruct((POOL_BUF, nb, dp), F32)],
        compiler_params=pltpu.CompilerParams(vmem_limit_bytes=VMEM_LIMIT_BYTES),
        name="mixer_sample",
    )(step_major(h.reshape(nb, nt, width)), step_major(state_conv), step_major(state_pool), wc, wp, sp)
    return step_major(z).reshape(rows, dc + dp), step_major(conv_new), step_major(pool_new)


def _neg_softplus(z):
    return -(jnp.maximum(z, 0.0) + jnp.log1p(jnp.exp(-jnp.abs(z))))


def _suffix_sum(ls, tri):
    hi = ls.astype(BF16)
    lo = (ls - hi.astype(F32)).astype(BF16)
    return (jnp.dot(hi, tri, preferred_element_type=F32)
            + jnp.dot(lo, tri, preferred_element_type=F32))


def _lower_tri(n):
    r = lax.broadcasted_iota(jnp.int32, (n, n), 0)
    c = lax.broadcasted_iota(jnp.int32, (n, n), 1)
    return jnp.where(r >= c, 1.0, 0.0).astype(BF16)


def _sb_prompt_kernel(ij_ref, bias_ref, qb_ref, kb_ref, vb_ref, o_ref,
                      zr_ref, z_ref, hi_ref, lo_ref, e_ref, c_ref, acc_ref, offset_ref, *, tb, scale):
    npairs = ij_ref.shape[1]
    bias = bias_ref[pl.program_id(1)]
    for ref in (zr_ref, z_ref, hi_ref, lo_ref, e_ref, c_ref, acc_ref):
        ref[...] = jnp.zeros(ref.shape, ref.dtype)
    tri = _lower_tri(tb)
    hidden = jnp.where(lax.broadcasted_iota(jnp.int32, (tb, tb), 1)
                       < lax.broadcasted_iota(jnp.int32, (tb, tb), 0), 0.0, MASKED_LOGIT)
    offset_ref[0] = jnp.full((tb, tb), bias, F32)
    offset_ref[1] = bias + hidden

    def pair(p):
        p = jnp.clip(p, 0, npairs - 1)
        return ij_ref[0, p], ij_ref[1, p]

    def block_rows(ref, blk):
        return ref[pl.ds(pl.multiple_of(blk * tb, tb), tb), :]

    def step(n, s):
        t = 1 - s
        i, j = pair(n)
        zr_ref[s] = lax.dot_general(block_rows(qb_ref, i), block_rows(kb_ref, j),
                                    (((1,), (1,)), ((), ())), preferred_element_type=F32)

        i, j = pair(n - 1)
        z = zr_ref[t] * scale + offset_ref[(i == j).astype(jnp.int32)]
        nl = jnp.maximum(z, 0.0) + jnp.log(1.0 + jnp.exp(-jnp.abs(z)))
        hi = nl.astype(BF16)
        z_ref[t] = z
        hi_ref[t] = hi
        lo_ref[t] = (nl - hi.astype(F32)).astype(BF16)

        i, j = pair(n - 2)
        rc = (jnp.dot(hi_ref[s], tri, preferred_element_type=F32)
              + jnp.dot(lo_ref[s], tri, preferred_element_type=F32))
        c = jnp.where(i == j, 0.0, c_ref[...])
        e_ref[s] = jnp.exp(z_ref[s] - rc - c).astype(BF16)
        c_ref[...] = c + rc[:, 0:1]

        i, j = pair(n - 3)
        acc = jnp.where(i == j, 0.0, acc_ref[...])
        new = acc + jnp.dot(e_ref[t], block_rows(vb_ref, j), preferred_element_type=F32)
        acc = jnp.where(n - 3 < npairs, new, acc)
        acc_ref[...] = acc
        o_ref[pl.ds(pl.multiple_of(i * tb, tb), tb), :] = acc.astype(o_ref.dtype)

    def trip(m, carry):
        for r in range(STEPS_PER_TRIP):
            step(STEPS_PER_TRIP * m + r, r % 2)
        return carry

    n_steps = npairs + 3
    lax.fori_loop(0, -(-n_steps // STEPS_PER_TRIP), trip, 0)


def sb_attention_prompt(q, k, v, bias, batch, seq, tb):
    nh, rows, dh = q.shape
    nblk = seq // tb
    pairs = [(i, j) for i in range(nblk) for j in range(i, -1, -1)]
    ij = jnp.asarray(pairs, dtype=jnp.int32).T
    blk = pl.BlockSpec((None, seq, dh), lambda b, h: (h, b, 0))
    smem = pl.BlockSpec(memory_space=pltpu.SMEM)
    stage_f32 = pltpu.VMEM((2, tb, tb), F32)
    stage_bf16 = pltpu.VMEM((2, tb, tb), BF16)
    return pl.pallas_call(
        functools.partial(_sb_prompt_kernel, tb=tb, scale=dh ** -0.5),
        grid=(batch, nh),
        in_specs=[smem, smem, blk, blk, blk],
        out_specs=blk,
        out_shape=jax.ShapeDtypeStruct((nh, rows, dh), BF16),
        scratch_shapes=[stage_f32, stage_f32, stage_bf16, stage_bf16, stage_bf16,
                        pltpu.VMEM((tb, 1), F32), pltpu.VMEM((tb, dh), F32), stage_f32],
        compiler_params=_params(2),
        name="sb_attention_prompt",
    )(ij, bias, q, k, v)


def _sb_sample_kernel(pt_ref, q_ref, kn_ref, vn_ref, bias_ref, *refs, n_pg, scale):
    del pt_ref
    nt, nh, dh = kn_ref.shape
    ngrp = nh // SUBLANES
    k_refs = refs[:n_pg * ngrp]
    v_refs = refs[n_pg * ngrp:2 * n_pg * ngrp]
    o_ref, qf_ref, c_ref, acc_ref = refs[2 * n_pg * ngrp:]

    def head_rows(page_refs, p, h):
        flat = page_refs[p * ngrp + h // SUBLANES].reshape(PAGE_SIZE * SUBLANES, dh)
        return flat[pl.ds(h % SUBLANES, PAGE_SIZE, stride=SUBLANES), :]

    rpad = qf_ref.shape[0] // nh
    g = pl.program_id(1)
    bias_col = bias_ref[...]

    @pl.when(g == 0)
    def _():
        for h in range(nh):
            qf_ref[h * rpad:(h + 1) * rpad, :] = q_ref[:, h * dh:(h + 1) * dh]
        rr = lax.broadcasted_iota(jnp.int32, (nh * rpad, nh), 0)
        hh = lax.broadcasted_iota(jnp.int32, (nh * rpad, nh), 1)
        rep = jnp.where(rr // rpad == hh, 1.0, 0.0).astype(BF16)
        step = lax.broadcasted_iota(jnp.int32, (nh * rpad, 1), 0) % rpad
        qr = qf_ref[...].astype(BF16).astype(F32)
        zs, lss, valids = [], [], []
        for s in range(nt):
            kr = jnp.dot(rep, kn_ref[s].astype(BF16), preferred_element_type=F32)
            z = jnp.sum(qr * kr, axis=1, keepdims=True) * scale + bias_col
            valid = step > s
            zs.append(z)
            valids.append(valid)
            lss.append(jnp.where(valid, _neg_softplus(z), 0.0))
        acc = jnp.zeros(acc_ref.shape, F32)
        rc = jnp.zeros((nh * rpad, 1), F32)
        for s in reversed(range(nt)):
            rc = rc + lss[s]
            a = jnp.where(valids[s], jnp.exp(zs[s] + rc), 0.0)
            vr = jnp.dot(rep, vn_ref[s].astype(BF16), preferred_element_type=F32)
            acc = acc + a.astype(BF16).astype(F32) * vr
        acc_ref[...] = acc
        c_ref[...] = rc

    bf16_values = lambda x: x.astype(BF16).astype(F32)
    width = n_pg * PAGE_SIZE
    zrows = []
    for h in range(nh):
        qh = bf16_values(qf_ref[h * rpad:(h + 1) * rpad, :])
        kh = jnp.concatenate([head_rows(k_refs, p, h) for p in range(n_pg)], axis=0)
        zrows.append(lax.dot_general(qh, kh, (((1,), (1,)), ((), ())), preferred_element_type=F32))
    z = jnp.concatenate(zrows, axis=0) * scale + bias_col
    ls = _neg_softplus(z)
    tb = 2 * PAGE_SIZE
    tri = _lower_tri(tb)
    c = c_ref[...]
    a_blocks = [None] * (width // tb)
    for blk in reversed(range(width // tb)):
        rc = _suffix_sum(ls[:, blk * tb:(blk + 1) * tb], tri)
        a_blocks[blk] = bf16_values(jnp.exp(z[:, blk * tb:(blk + 1) * tb] + rc + c))
        c = c + rc[:, 0:1]
    c_ref[...] = c
    a = jnp.concatenate(a_blocks, axis=1)
    orows = []
    for h in range(nh):
        vh = jnp.concatenate([head_rows(v_refs, p, h) for p in range(n_pg)], axis=0)
        orows.append(jnp.dot(a[h * rpad:(h + 1) * rpad, :], vh, preferred_element_type=F32))
    acc_ref[...] += jnp.concatenate(orows, axis=0)

    @pl.when(g == pl.num_programs(1) - 1)
    def _():
        for h in range(nh):
            o_ref[:, h * dh:(h + 1) * dh] = acc_ref[h * rpad:(h + 1) * rpad, :]


def sb_attention_sample(q, k_new, v_new, bias, cache_k, cache_v, page_table, n_pg):
    nb, nt, d_attn = q.shape
    nh, dh = k_new.shape[2], k_new.shape[3]
    n_pages = page_table.shape[1]
    rpad = SUBLANES
    q_pad = jnp.pad(q, ((0, 0), (0, rpad - nt), (0, 0)))
    bias_col = jnp.repeat(bias.astype(F32), rpad).reshape(nh * rpad, 1)
    n_steps = n_pages // n_pg

    ngrp = nh // SUBLANES
    cache_k = cache_k.reshape(-1, PAGE_SIZE, ngrp, SUBLANES, dh)
    cache_v = cache_v.reshape(-1, PAGE_SIZE, ngrp, SUBLANES, dh)

    def page_spec(p, grp):
        return pl.BlockSpec(
            (None, PAGE_SIZE, None, SUBLANES, dh),
            lambda b, g, pt: (pt[b, n_pages - (g + 1) * n_pg + p], 0, grp, 0, 0))

    page_specs = [page_spec(p, grp) for p in range(n_pg) for grp in range(ngrp)]

    per_seq3 = lambda b, g, pt: (b, 0, 0)
    per_seq4 = lambda b, g, pt: (b, 0, 0, 0)
    grid_spec = pltpu.PrefetchScalarGridSpec(
        num_scalar_prefetch=1,
        grid=(nb, n_steps),
        in_specs=[pl.BlockSpec((None, rpad, d_attn), per_seq3),
                  pl.BlockSpec((None, nt, nh, dh), per_seq4),
                  pl.BlockSpec((None, nt, nh, dh), per_seq4),
                  pl.BlockSpec((nh * rpad, 1), lambda b, g, pt: (0, 0))]
                 + page_specs * 2,
        out_specs=pl.BlockSpec((None, rpad, d_attn), per_seq3),
        scratch_shapes=[pltpu.VMEM((nh * rpad, dh), F32), pltpu.VMEM((nh * rpad, 1), F32),
                        pltpu.VMEM((nh * rpad, dh), F32)],
    )
    return pl.pallas_call(
        functools.partial(_sb_sample_kernel, n_pg=n_pg, scale=dh ** -0.5),
        grid_spec=grid_spec,
        out_shape=jax.ShapeDtypeStruct((nb, rpad, d_attn), F32),
        compiler_params=_params(2),
        name="sb_attention_sample",
    )(page_table, q_pad, k_new, v_new, bias_col,
      *([cache_k] * len(page_specs)), *([cache_v] * len(page_specs)))


def kernel(x_prompt, x_sample, state_conv, state_pool, cache_k, cache_v, page_table, ln_g, ln_b, mix_w_in, conv_w, pool_w, pool_scale, mix_w_out, attn_w_qkv, attn_w_o, attn_bias, ffn_w_gate, ffn_w_up, ffn_w_down):
    bp, seq, d = x_prompt.shape
    bs, ts_, _ = x_sample.shape
    mp, ms = bp * seq, bs * ts_
    past_len = page_table.shape[1] * PAGE_SIZE
    xp = x_prompt.reshape(mp, d)
    xs = x_sample.reshape(ms, d)
    dp = pool_scale.shape[1]
    dc = conv_w.shape[2]
    d_attn = attn_w_o.shape[1]
    dh = d_attn // N_HEADS
    ln_g = ln_g.reshape(-1, 1, d)
    ln_b = ln_b.reshape(-1, 1, d)

    def ffn(xp, xpb, xs, xsb, layer):
        hp, hs, wd = ffn_up(xpb, xsb, ffn_w_gate, ffn_w_up, ffn_w_down, layer, 1024, 512, f"ffn_up_l{layer}")
        return matmul_residual_ln(hp, hs, wd, 0, xp, xs, ln_g, ln_b, 2 * layer + 1, 256,
                                  f"ffn_down_ln_l{layer}")

    wp = pool_w[0].astype(BF16)
    sp = pool_scale[0].reshape(1, dp)
    hp, hs = matmul(xp, xs, mix_w_in, 0, 0, mix_w_in.shape[2], F32, 1024, 1024, "mix_in")
    zp, conv_p, pool_p = mixer_prompt(hp, conv_w[0], wp, sp, bp, seq, 512)
    zs, conv_s, pool_s = mixer_sample(hs, state_conv[0], state_pool[0], conv_w[0], wp, sp, bs, ts_, past_len)
    xp, xpb, xs, xsb = matmul_residual_ln(zp, zs, mix_w_out, 0, xp, xs, ln_g, ln_b, 0, 512, "mix_out_ln")
    xp, xpb, xs, xsb = ffn(xp, xpb, xs, xsb, 0)

    bias = attn_bias[0]
    qh, qs = matmul_heads(xpb, xsb, attn_w_qkv, 0, 0, d_attn, dh, 1024, d_attn, "q_proj", False)
    kp, kh, ks = matmul_heads(xpb, xsb, attn_w_qkv, 0, d_attn, d_attn, dh, 512, d_attn, "k_proj", True)
    vp, vh, vs = matmul_heads(xpb, xsb, attn_w_qkv, 0, 2 * d_attn, d_attn, dh, 1024, 1024, "v_proj", True)
    op = sb_attention_prompt(qh, kh, vh, bias, bp, seq, 256)
    ks4 = ks.reshape(bs, ts_, N_HEADS, dh)
    vs4 = vs.reshape(bs, ts_, N_HEADS, dh)
    o_pad = sb_attention_sample(qs.reshape(bs, ts_, d_attn), ks4, vs4, bias, cache_k[0], cache_v[0],
                                page_table, 8)
    os_ = o_pad[:, :ts_, :].reshape(ms, d_attn)
    xp, xpb, xs, xsb = matmul_residual_ln(op, os_, attn_w_o, 0, xp, xs, ln_g, ln_b, 2, 512, "attn_out_ln")
    xp, _, xs, _ = ffn(xp, xpb, xs, xsb, 1)

    return (xp.reshape(bp, seq, d), xs.reshape(bs, ts_, d),
            conv_p[None], conv_s[None], pool_p[None], pool_s[None],
            kp.reshape(1, bp, seq, N_HEADS, dh), vp.reshape(1, bp, seq, N_HEADS, dh),
            ks4[None], vs4[None])
```

```python
import functools

import jax
import jax.numpy as jnp
from jax import lax
from jax.experimental import pallas as pl
from jax.experimental.pallas import tpu as pltpu

F32 = jnp.float32
BF16 = jnp.bfloat16

CONV_WIDTH = 3
POOL_WINDOWS = (2, 4, 8, 16)
POOL_BUF = max(POOL_WINDOWS) - 1
N_HEADS = 16
PAGE_SIZE = 128
DEPTH = 2
ALPHA = (2.0 * DEPTH) ** 0.25
LN_EPS = 1e-5
MASKED_LOGIT = -1e30
STEPS_PER_TRIP = 40

VMEM_LIMIT_BYTES = 60 * 1024 * 1024
SUBLANES = 8


def _params(n_axes):
    return pltpu.CompilerParams(
        dimension_semantics=("arbitrary",) * n_axes,
        vmem_limit_bytes=VMEM_LIMIT_BYTES,
    )


def _dot(x, w):
    return jnp.dot(x.astype(BF16), w, preferred_element_type=F32)


def _mm_kernel(xm_ref, xe_ref, w_ref, om_ref, oe_ref, wb_ref):
    @pl.when(pl.program_id(1) == 0)
    def _():
        wb_ref[...] = w_ref[...].astype(BF16)
        oe_ref[...] = _dot(xe_ref[...], wb_ref[...]).astype(oe_ref.dtype)

    om_ref[...] = _dot(xm_ref[...], wb_ref[...]).astype(om_ref.dtype)


def matmul(xm, xe, w, layer, col0, n, out_dtype, tm, tn, name):
    m, k = xm.shape
    me = xe.shape[0]
    cb0 = col0 // tn
    return pl.pallas_call(
        _mm_kernel,
        grid=(n // tn, m // tm),
        in_specs=[pl.BlockSpec((tm, k), lambda j, i: (i, 0)),
                  pl.BlockSpec((me, k), lambda j, i: (0, 0)),
                  pl.BlockSpec((None, k, tn), lambda j, i: (layer, 0, cb0 + j))],
        out_specs=[pl.BlockSpec((tm, tn), lambda j, i: (i, j)),
                   pl.BlockSpec((me, tn), lambda j, i: (0, j))],
        out_shape=[jax.ShapeDtypeStruct((m, n), out_dtype), jax.ShapeDtypeStruct((me, n), out_dtype)],
        scratch_shapes=[pltpu.VMEM((k, tn), BF16)],
        compiler_params=_params(2),
        name=name,
    )(xm, xe, w)


def _mm_heads_kernel(xm_ref, xe_ref, w_ref, *refs, dh, row_major):
    om_ref = refs[0] if row_major else None
    oh_ref, oe_ref, wb_ref = refs[-3:]

    @pl.when(pl.program_id(1) == 0)
    def _():
        wb_ref[...] = w_ref[...].astype(BF16)
        oe_ref[...] = _dot(xe_ref[...], wb_ref[...])

    res = _dot(xm_ref[...], wb_ref[...])
    if row_major:
        om_ref[...] = res
    for h in range(oh_ref.shape[0]):
        oh_ref[h] = res[:, h * dh:(h + 1) * dh].astype(BF16)


def matmul_heads(xm, xe, w, layer, col0, n, dh, tm, tn, name, row_major):
    m, k = xm.shape
    me = xe.shape[0]
    cb0 = col0 // tn
    hpt = tn // dh
    out_specs = [pl.BlockSpec((hpt, tm, dh), lambda j, i: (j, i, 0)),
                 pl.BlockSpec((me, tn), lambda j, i: (0, j))]
    out_shape = [jax.ShapeDtypeStruct((n // dh, m, dh), BF16), jax.ShapeDtypeStruct((me, n), F32)]
    if row_major:
        out_specs.insert(0, pl.BlockSpec((tm, tn), lambda j, i: (i, j)))
        out_shape.insert(0, jax.ShapeDtypeStruct((m, n), F32))
    w_mode = {"pipeline_mode": pl.Buffered(1)} if n == tn else {}
    return pl.pallas_call(
        functools.partial(_mm_heads_kernel, dh=dh, row_major=row_major),
        grid=(n // tn, m // tm),
        in_specs=[pl.BlockSpec((tm, k), lambda j, i: (i, 0)),
                  pl.BlockSpec((me, k), lambda j, i: (0, 0)),
                  pl.BlockSpec((None, k, tn), lambda j, i: (layer, 0, cb0 + j), **w_mode)],
        out_specs=out_specs,
        out_shape=out_shape,
        scratch_shapes=[pltpu.VMEM((k, tn), BF16)],
        compiler_params=_params(2),
        name=name,
    )(xm, xe, w)


def _layer_norm_rows(s, g, b):
    mu = jnp.mean(s, axis=-1, keepdims=True)
    d = s - mu
    var = jnp.mean(d * d, axis=-1, keepdims=True)
    return d * lax.rsqrt(var + LN_EPS) * g + b


def _mm_ln_kernel(xm_ref, xe_ref, w_ref, rm_ref, re_ref, g_ref, b_ref, *refs, bf16_copy):
    if bf16_copy:
        ym_ref, ymb_ref, ye_ref, yeb_ref = refs[:4]
    else:
        ym_ref, ye_ref = refs[:2]
        ymb_ref = yeb_ref = None
    wb_refs = refs[4 if bf16_copy else 2:]
    wb_ref = wb_refs[0] if wb_refs else w_ref

    def rows(x_ref, r_ref, y_ref, yb_ref):
        if len(x_ref.shape) == 3:
            x = jnp.concatenate([x_ref[h] for h in range(x_ref.shape[0])], axis=1)
        else:
            x = x_ref[...]
        y = _layer_norm_rows(ALPHA * r_ref[...] + _dot(x, wb_ref[...]), g_ref[...], b_ref[...])
        y_ref[...] = y
        if yb_ref is not None:
            yb_ref[...] = y.astype(BF16)

    @pl.when(pl.program_id(0) == 0)
    def _():
        if wb_refs:
            wb_ref[...] = w_ref[...].astype(BF16)
        rows(xe_ref, re_ref, ye_ref, yeb_ref)

    rows(xm_ref, rm_ref, ym_ref, ymb_ref)


def matmul_residual_ln(xm, xe, w, layer, rm, re, ln_g, ln_b, ln_row, tm, name, bf16_copy=True):
    me, k = xe.shape
    m = xm.shape[-2]
    d = w.shape[-1]
    row = lambda i: (i, 0)
    const = lambda i: (0, 0)
    if xm.ndim == 3:
        xm_spec = pl.BlockSpec((xm.shape[0], tm, xm.shape[2]), lambda i: (0, i, 0))
    else:
        xm_spec = pl.BlockSpec((tm, k), row)
    if w.ndim == 2:
        w_spec = pl.BlockSpec((k, d), const, pipeline_mode=pl.Buffered(1))
        scratch = []
    else:
        w_spec = pl.BlockSpec((None, k, d), lambda i: (layer, 0, 0), pipeline_mode=pl.Buffered(1))
        scratch = [pltpu.VMEM((k, d), BF16)]
    ln_spec = pl.BlockSpec((None, 1, d), lambda i: (ln_row, 0, 0))
    dtypes = (F32, BF16) if bf16_copy else (F32,)
    outs = pl.pallas_call(
        functools.partial(_mm_ln_kernel, bf16_copy=bf16_copy),
        grid=(m // tm,),
        in_specs=[xm_spec, pl.BlockSpec((me, k), const), w_spec,
                  pl.BlockSpec((tm, d), row), pl.BlockSpec((me, d), const), ln_spec, ln_spec],
        out_specs=[pl.BlockSpec((tm, d), row)] * len(dtypes) + [pl.BlockSpec((me, d), const)] * len(dtypes),
        out_shape=[jax.ShapeDtypeStruct((m, d), dt) for dt in dtypes]
                  + [jax.ShapeDtypeStruct((me, d), dt) for dt in dtypes],
        scratch_shapes=scratch,
        compiler_params=_params(1),
        name=name,
    )(xm, xe, w, rm, re, ln_g, ln_b)
    return outs if bf16_copy else (outs[0], None, outs[1], None)


def _ffn_up_kernel(xm_ref, xe_ref, wg_ref, wu_ref, wd_ref, hm_ref, he_ref, wdb_ref, wgb_ref, wub_ref):
    def rows(x_ref, h_ref):
        x = x_ref[...]
        h_ref[...] = (jax.nn.silu(_dot(x, wgb_ref[...])) * _dot(x, wub_ref[...])).astype(h_ref.dtype)

    @pl.when(pl.program_id(1) == 0)
    def _():
        wgb_ref[...] = wg_ref[...].astype(BF16)
        wub_ref[...] = wu_ref[...].astype(BF16)
        rows(xe_ref, he_ref)

    wdb_ref[...] = wd_ref[...].astype(BF16)
    rows(xm_ref, hm_ref)


def ffn_up(xm, xe, wg, wu, wd, layer, tm, tf, name):
    m, k = xm.shape
    me = xe.shape[0]
    f = wg.shape[2]
    d = wd.shape[2]
    n_row_steps = m // tm
    slab = tf // n_row_steps
    col = lambda j, i: (layer, 0, j)
    return pl.pallas_call(
        _ffn_up_kernel,
        grid=(f // tf, m // tm),
        in_specs=[pl.BlockSpec((tm, k), lambda j, i: (i, 0)),
                  pl.BlockSpec((me, k), lambda j, i: (0, 0)),
                  pl.BlockSpec((None, k, tf), col), pl.BlockSpec((None, k, tf), col),
                  pl.BlockSpec((None, slab, d), lambda j, i: (layer, j * n_row_steps + i, 0))],
        out_specs=[pl.BlockSpec((tm, tf), lambda j, i: (i, j)),
                   pl.BlockSpec((me, tf), lambda j, i: (0, j)),
                   pl.BlockSpec((slab, d), lambda j, i: (j * n_row_steps + i, 0))],
        out_shape=[jax.ShapeDtypeStruct((m, f), BF16), jax.ShapeDtypeStruct((me, f), BF16),
                   jax.ShapeDtypeStruct((f, d), BF16)],
        scratch_shapes=[pltpu.VMEM((k, tf), BF16), pltpu.VMEM((k, tf), BF16)],
        compiler_params=_params(2),
        name=name,
    )(xm, xe, wg, wu, wd)


def _mixer_prompt_kernel(h_ref, wc_ref, wp_ref, sp_ref, z_ref, co_ref, po_ref, cbuf, pbuf, *, ts, dc):
    i = pl.program_id(1)
    ch, ph = SUBLANES, 2 * SUBLANES
    dp = pbuf.shape[1]
    pg = dp // len(POOL_WINDOWS)

    @pl.when(i == 0)
    def _():
        cbuf[0:ch, :] = jnp.zeros((ch, dc), F32)
        pbuf[0:ph, :] = jnp.zeros((ph, dp), F32)

    gb = h_ref[:, 0:dc]
    u = h_ref[:, dc:2 * dc] * h_ref[:, 2 * dc:3 * dc]
    pv = h_ref[:, 3 * dc:3 * dc + dp]
    cbuf[ch:ch + ts, :] = u
    pbuf[ph:ph + ts, :] = pv

    y = cbuf[ch - 2:ch - 2 + ts, :] * wc_ref[0:1, :]
    y = y + cbuf[ch - 1:ch - 1 + ts, :] * wc_ref[1:2, :]
    y = y + u * wc_ref[2:3, :]
    z_ref[:, 0:dc] = (gb * y).astype(z_ref.dtype)

    pos = i * ts + lax.broadcasted_iota(jnp.int32, (ts, 1), 0)
    for gi, w in enumerate(POOL_WINDOWS):
        lo = gi * pg
        cur = pv[:, lo:lo + pg]
        s = cur
        for back in range(1, w):
            s = s + pbuf[ph - back:ph - back + ts, lo:lo + pg]
        cnt = jnp.minimum(w, pos + 1).astype(F32)
        dlt = s / cnt - cur
        yg = jnp.dot(dlt.astype(BF16), wp_ref[gi], preferred_element_type=F32)
        z_ref[:, dc + lo:dc + lo + pg] = (yg * sp_ref[:, lo:lo + pg]).astype(z_ref.dtype)

    co_ref[...] = cbuf[ch + ts - (CONV_WIDTH - 1):ch + ts, :]
    po_ref[...] = pbuf[ph + ts - POOL_BUF:ph + ts, :]
    cbuf[0:ch, :] = cbuf[ts:ts + ch, :]
    pbuf[0:ph, :] = pbuf[ts:ts + ph, :]


def mixer_prompt(h, wc, wp, sp, batch, seq, ts):
    rows, width = h.shape
    dp = sp.shape[1]
    dc = (width - dp) // 3
    nt = seq // ts
    return pl.pallas_call(
        functools.partial(_mixer_prompt_kernel, ts=ts, dc=dc),
        grid=(batch, nt),
        in_specs=[pl.BlockSpec((ts, width), lambda b, i: (b * nt + i, 0)),
                  pl.BlockSpec(wc.shape, lambda b, i: (0, 0)),
                  pl.BlockSpec(wp.shape, lambda b, i: (0, 0, 0)),
                  pl.BlockSpec(sp.shape, lambda b, i: (0, 0))],
        out_specs=[pl.BlockSpec((ts, dc + dp), lambda b, i: (b * nt + i, 0)),
                   pl.BlockSpec((None, CONV_WIDTH - 1, dc), lambda b, i: (b, 0, 0)),
                   pl.BlockSpec((None, POOL_BUF, dp), lambda b, i: (b, 0, 0))],
        out_shape=[jax.ShapeDtypeStruct((rows, dc + dp), BF16),
                   jax.ShapeDtypeStruct((batch, CONV_WIDTH - 1, dc), F32),
                   jax.ShapeDtypeStruct((batch, POOL_BUF, dp), F32)],
        scratch_shapes=[pltpu.VMEM((ts + SUBLANES, dc), F32), pltpu.VMEM((ts + 2 * SUBLANES, dp), F32)],
        compiler_params=_params(2),
        name="mixer_prompt",
    )(h, wc, wp, sp)


def _mixer_sample_kernel(h_ref, sc_ref, spool_ref, wc_ref, wp_ref, sp_ref, z_ref, co_ref, po_ref,
                         *, nb, nt, dc, pos0):
    dp = sp_ref.shape[1]
    pg = dp // len(POOL_WINDOWS)
    nc = CONV_WIDTH - 1

    conv_ext = [sc_ref[j] for j in range(nc)]
    pool_ext = [spool_ref[j] for j in range(POOL_BUF)]
    gbs = []
    for t in range(nt):
        gbs.append(h_ref[t, :, 0:dc])
        conv_ext.append(h_ref[t, :, dc:2 * dc] * h_ref[t, :, 2 * dc:3 * dc])
        pool_ext.append(h_ref[t, :, 3 * dc:3 * dc + dp])

    for t in range(nt):
        y = conv_ext[t] * wc_ref[0:1, :]
        for kk in range(1, CONV_WIDTH):
            y = y + conv_ext[t + kk] * wc_ref[kk:kk + 1, :]
        z_ref[t, :, 0:dc] = gbs[t] * y
    for j in range(nc):
        co_ref[j] = conv_ext[nt + j]

    for gi, w in enumerate(POOL_WINDOWS):
        lo = gi * pg
        dls = []
        for t in range(nt):
            e = POOL_BUF + t
            s = pool_ext[e][:, lo:lo + pg]
            for back in range(1, w):
                s = s + pool_ext[e - back][:, lo:lo + pg]
            cnt = float(min(w, pos0 + t + 1))
            dls.append(s / cnt - pool_ext[e][:, lo:lo + pg])
        dl = jnp.concatenate(dls, axis=0).astype(BF16)
        yg = jnp.dot(dl, wp_ref[gi], preferred_element_type=F32) * sp_ref[:, lo:lo + pg]
        for t in range(nt):
            z_ref[t, :, dc + lo:dc + lo + pg] = yg[t * nb:(t + 1) * nb, :]
    for j in range(POOL_BUF):
        po_ref[j] = pool_ext[nt + j]


def mixer_sample(h, state_conv, state_pool, wc, wp, sp, nb, nt, pos0):
    rows, width = h.shape
    dp = sp.shape[1]
    dc = (width - dp) // 3
    nc = CONV_WIDTH - 1
    step_major = lambda a: jnp.swapaxes(a, 0, 1)
    z, conv_new, pool_new = pl.pallas_call(
        functools.partial(_mixer_sample_kernel, nb=nb, nt=nt, dc=dc, pos0=pos0),
        out_shape=[jax.ShapeDtypeStruct((nt, nb, dc + dp), F32),
                   jax.ShapeDtypeStruct((nc, nb, dc), F32),
                   jax.ShapeDtypeStruct((POOL_BUF, nb, dp), F32)],
        compiler_params=pltpu.CompilerParams(vmem_limit_bytes=VMEM_LIMIT_BYTES),
        name="mixer_sample",
    )(step_major(h.reshape(nb, nt, width)), step_major(state_conv), step_major(state_pool), wc, wp, sp)
    return step_major(z).reshape(rows, dc + dp), step_major(conv_new), step_major(pool_new)


def _neg_softplus(z):
    return -(jnp.maximum(z, 0.0) + jnp.log1p(jnp.exp(-jnp.abs(z))))


def _suffix_sum(ls, tri):
    hi = ls.astype(BF16)
    lo = (ls - hi.astype(F32)).astype(BF16)
    return (jnp.dot(hi, tri, preferred_element_type=F32)
            + jnp.dot(lo, tri, preferred_element_type=F32))


def _lower_tri(n):
    r = lax.broadcasted_iota(jnp.int32, (n, n), 0)
    c = lax.broadcasted_iota(jnp.int32, (n, n), 1)
    return jnp.where(r >= c, 1.0, 0.0).astype(BF16)


def _sb_prompt_kernel(ij_ref, bias_ref, qb_ref, kb_ref, vb_ref, o_ref,
                      zr_ref, z_ref, hi_ref, lo_ref, e_ref, c_ref, acc_ref, offset_ref, *, tb, scale):
    npairs = ij_ref.shape[1]
    bias = bias_ref[pl.program_id(1)]
    for ref in (zr_ref, z_ref, hi_ref, lo_ref, e_ref, c_ref, acc_ref):
        ref[...] = jnp.zeros(ref.shape, ref.dtype)
    tri = _lower_tri(tb)
    hidden = jnp.where(lax.broadcasted_iota(jnp.int32, (tb, tb), 1)
                       < lax.broadcasted_iota(jnp.int32, (tb, tb), 0), 0.0, MASKED_LOGIT)
    offset_ref[0] = jnp.full((tb, tb), bias, F32)
    offset_ref[1] = bias + hidden

    def pair(p):
        p = jnp.clip(p, 0, npairs - 1)
        return ij_ref[0, p], ij_ref[1, p]

    def block_rows(ref, blk):
        return ref[pl.ds(pl.multiple_of(blk * tb, tb), tb), :]

    def step(n, s):
        t = 1 - s
        i, j = pair(n)
        zr_ref[s] = lax.dot_general(block_rows(qb_ref, i), block_rows(kb_ref, j),
                                    (((1,), (1,)), ((), ())), preferred_element_type=F32)

        i, j = pair(n - 1)
        z = zr_ref[t] * scale + offset_ref[(i == j).astype(jnp.int32)]
        nl = jnp.maximum(z, 0.0) + jnp.log(1.0 + jnp.exp(-jnp.abs(z)))
        hi = nl.astype(BF16)
        z_ref[t] = z
        hi_ref[t] = hi
        lo_ref[t] = (nl - hi.astype(F32)).astype(BF16)

        i, j = pair(n - 2)
        rc = (jnp.dot(hi_ref[s], tri, preferred_element_type=F32)
              + jnp.dot(lo_ref[s], tri, preferred_element_type=F32))
        c = jnp.where(i == j, 0.0, c_ref[...])
        e_ref[s] = jnp.exp(z_ref[s] - rc - c).astype(BF16)
        c_ref[...] = c + rc[:, 0:1]

        i, j = pair(n - 3)
        acc = jnp.where(i == j, 0.0, acc_ref[...])
        new = acc + jnp.dot(e_ref[t], block_rows(vb_ref, j), preferred_element_type=F32)
        acc = jnp.where(n - 3 < npairs, new, acc)
        acc_ref[...] = acc
        o_ref[pl.ds(pl.multiple_of(i * tb, tb), tb), :] = acc.astype(o_ref.dtype)

    def trip(m, carry):
        for r in range(STEPS_PER_TRIP):
            step(STEPS_PER_TRIP * m + r, r % 2)
        return carry

    n_steps = npairs + 3
    lax.fori_loop(0, -(-n_steps // STEPS_PER_TRIP), trip, 0)


def sb_attention_prompt(q, k, v, bias, batch, seq, tb):
    nh, rows, dh = q.shape
    nblk = seq // tb
    pairs = [(i, j) for i in range(nblk) for j in range(i, -1, -1)]
    ij = jnp.asarray(pairs, dtype=jnp.int32).T
    blk = pl.BlockSpec((None, seq, dh), lambda b, h: (h, b, 0))
    smem = pl.BlockSpec(memory_space=pltpu.SMEM)
    stage_f32 = pltpu.VMEM((2, tb, tb), F32)
    stage_bf16 = pltpu.VMEM((2, tb, tb), BF16)
    return pl.pallas_call(
        functools.partial(_sb_prompt_kernel, tb=tb, scale=dh ** -0.5),
        grid=(batch, nh),
        in_specs=[smem, smem, blk, blk, blk],
        out_specs=blk,
        out_shape=jax.ShapeDtypeStruct((nh, rows, dh), BF16),
        scratch_shapes=[stage_f32, stage_f32, stage_bf16, stage_bf16, stage_bf16,
                        pltpu.VMEM((tb, 1), F32), pltpu.VMEM((tb, dh), F32), stage_f32],
        compiler_params=_params(2),
        name="sb_attention_prompt",
    )(ij, bias, q, k, v)


def _sb_sample_kernel(pt_ref, q_ref, kn_ref, vn_ref, bias_ref, *refs, n_pg, scale):
    del pt_ref
    nt, nh, dh = kn_ref.shape
    ngrp = nh // SUBLANES
    k_refs = refs[:n_pg * ngrp]
    v_refs = refs[n_pg * ngrp:2 * n_pg * ngrp]
    o_ref, qf_ref, c_ref, acc_ref = refs[2 * n_pg * ngrp:]

    def head_rows(page_refs, p, h):
        flat = page_refs[p * ngrp + h // SUBLANES].reshape(PAGE_SIZE * SUBLANES, dh)
        return flat[pl.ds(h % SUBLANES, PAGE_SIZE, stride=SUBLANES), :]

    rpad = qf_ref.shape[0] // nh
    g = pl.program_id(1)
    bias_col = bias_ref[...]

    @pl.when(g == 0)
    def _():
        for h in range(nh):
            qf_ref[h * rpad:(h + 1) * rpad, :] = q_ref[:, h * dh:(h + 1) * dh]
        rr = lax.broadcasted_iota(jnp.int32, (nh * rpad, nh), 0)
        hh = lax.broadcasted_iota(jnp.int32, (nh * rpad, nh), 1)
        rep = jnp.where(rr // rpad == hh, 1.0, 0.0).astype(BF16)
        step = lax.broadcasted_iota(jnp.int32, (nh * rpad, 1), 0) % rpad
        qr = qf_ref[...].astype(BF16).astype(F32)
        zs, lss, valids = [], [], []
        for s in range(nt):
            kr = jnp.dot(rep, kn_ref[s].astype(BF16), preferred_element_type=F32)
            z = jnp.sum(qr * kr, axis=1, keepdims=True) * scale + bias_col
            valid = step > s
            zs.append(z)
            valids.append(valid)
            lss.append(jnp.where(valid, _neg_softplus(z), 0.0))
        acc = jnp.zeros(acc_ref.shape, F32)
        rc = jnp.zeros((nh * rpad, 1), F32)
        for s in reversed(range(nt)):
            rc = rc + lss[s]
            a = jnp.where(valids[s], jnp.exp(zs[s] + rc), 0.0)
            vr = jnp.dot(rep, vn_ref[s].astype(BF16), preferred_element_type=F32)
            acc = acc + a.astype(BF16).astype(F32) * vr
        acc_ref[...] = acc
        c_ref[...] = rc

    bf16_values = lambda x: x.astype(BF16).astype(F32)
    width = n_pg * PAGE_SIZE
    zrows = []
    for h in range(nh):
        qh = bf16_values(qf_ref[h * rpad:(h + 1) * rpad, :])
        kh = jnp.concatenate([head_rows(k_refs, p, h) for p in range(n_pg)], axis=0)
        zrows.append(lax.dot_general(qh, kh, (((1,), (1,)), ((), ())), preferred_element_type=F32))
    z = jnp.concatenate(zrows, axis=0) * scale + bias_col
    ls = _neg_softplus(z)
    tb = 2 * PAGE_SIZE
    tri = _lower_tri(tb)
    c = c_ref[...]
    a_blocks = [None] * (width // tb)
    for blk in reversed(range(width // tb)):
        rc = _suffix_sum(ls[:, blk * tb:(blk + 1) * tb], tri)
        a_blocks[blk] = bf16_values(jnp.exp(z[:, blk * tb:(blk + 1) * tb] + rc + c))
        c = c + rc[:, 0:1]
    c_ref[...] = c
    a = jnp.concatenate(a_blocks, axis=1)
    orows = []
    for h in range(nh):
        vh = jnp.concatenate([head_rows(v_refs, p, h) for p in range(n_pg)], axis=0)
        orows.append(jnp.dot(a[h * rpad:(h + 1) * rpad, :], vh, preferred_element_type=F32))
    acc_ref[...] += jnp.concatenate(orows, axis=0)

    @pl.when(g == pl.num_programs(1) - 1)
    def _():
        for h in range(nh):
            o_ref[:, h * dh:(h + 1) * dh] = acc_ref[h * rpad:(h + 1) * rpad, :]


def sb_attention_sample(q, k_new, v_new, bias, cache_k, cache_v, page_table, n_pg):
    nb, nt, d_attn = q.shape
    nh, dh = k_new.shape[2], k_new.shape[3]
    n_pages = page_table.shape[1]
    rpad = SUBLANES
    q_pad = jnp.pad(q, ((0, 0), (0, rpad - nt), (0, 0)))
    bias_col = jnp.repeat(bias.astype(F32), rpad).reshape(nh * rpad, 1)
    n_steps = n_pages // n_pg

    ngrp = nh // SUBLANES
    cache_k = cache_k.reshape(-1, PAGE_SIZE, ngrp, SUBLANES, dh)
    cache_v = cache_v.reshape(-1, PAGE_SIZE, ngrp, SUBLANES, dh)

    def page_spec(p, grp):
        return pl.BlockSpec(
            (None, PAGE_SIZE, None, SUBLANES, dh),
            lambda b, g, pt: (pt[b, n_pages - (g + 1) * n_pg + p], 0, grp, 0, 0))

    page_specs = [page_spec(p, grp) for p in range(n_pg) for grp in range(ngrp)]

    per_seq3 = lambda b, g, pt: (b, 0, 0)
    per_seq4 = lambda b, g, pt: (b, 0, 0, 0)
    grid_spec = pltpu.PrefetchScalarGridSpec(
        num_scalar_prefetch=1,
        grid=(nb, n_steps),
        in_specs=[pl.BlockSpec((None, rpad, d_attn), per_seq3),
                  pl.BlockSpec((None, nt, nh, dh), per_seq4),
                  pl.BlockSpec((None, nt, nh, dh), per_seq4),
                  pl.BlockSpec((nh * rpad, 1), lambda b, g, pt: (0, 0))]
                 + page_specs * 2,
        out_specs=pl.BlockSpec((None, rpad, d_attn), per_seq3),
        scratch_shapes=[pltpu.VMEM((nh * rpad, dh), F32), pltpu.VMEM((nh * rpad, 1), F32),
                        pltpu.VMEM((nh * rpad, dh), F32)],
    )
    return pl.pallas_call(
        functools.partial(_sb_sample_kernel, n_pg=n_pg, scale=dh ** -0.5),
        grid_spec=grid_spec,
        out_shape=jax.ShapeDtypeStruct((nb, rpad, d_attn), F32),
        compiler_params=_params(2),
        name="sb_attention_sample",
    )(page_table, q_pad, k_new, v_new, bias_col,
      *([cache_k] * len(page_specs)), *([cache_v] * len(page_specs)))


def kernel(x_prompt, x_sample, state_conv, state_pool, cache_k, cache_v, page_table, ln_g, ln_b, mix_w_in, conv_w, pool_w, pool_scale, mix_w_out, attn_w_qkv, attn_w_o, attn_bias, ffn_w_gate, ffn_w_up, ffn_w_down):
    bp, seq, d = x_prompt.shape
    bs, ts_, _ = x_sample.shape
    mp, ms = bp * seq, bs * ts_
    past_len = page_table.shape[1] * PAGE_SIZE
    xp = x_prompt.reshape(mp, d)
    xs = x_sample.reshape(ms, d)
    dp = pool_scale.shape[1]
    dc = conv_w.shape[2]
    d_attn = attn_w_o.shape[1]
    dh = d_attn // N_HEADS
    ln_g = ln_g.reshape(-1, 1, d)
    ln_b = ln_b.reshape(-1, 1, d)

    def ffn(xp, xpb, xs, xsb, layer):
        hp, hs, wd = ffn_up(xpb, xsb, ffn_w_gate, ffn_w_up, ffn_w_down, layer, 1024, 512, f"ffn_up_l{layer}")
        last = layer == DEPTH - 1
        return matmul_residual_ln(hp, hs, wd, 0, xp, xs, ln_g, ln_b, 2 * layer + 1, 256,
                                  f"ffn_down_ln_l{layer}", bf16_copy=not last)

    wp = pool_w[0].astype(BF16)
    sp = pool_scale[0].reshape(1, dp)
    hp, hs = matmul(xp, xs, mix_w_in, 0, 0, mix_w_in.shape[2], F32, 1024, 1024, "mix_in")
    zp, conv_p, pool_p = mixer_prompt(hp, conv_w[0], wp, sp, bp, seq, 1024)
    zs, conv_s, pool_s = mixer_sample(hs, state_conv[0], state_pool[0], conv_w[0], wp, sp, bs, ts_, past_len)
    xp, xpb, xs, xsb = matmul_residual_ln(zp, zs, mix_w_out, 0, xp, xs, ln_g, ln_b, 0, 512, "mix_out_ln")
    xp, xpb, xs, xsb = ffn(xp, xpb, xs, xsb, 0)

    bias = attn_bias[0]
    qh, qs = matmul_heads(xpb, xsb, attn_w_qkv, 0, 0, d_attn, dh, 1024, d_attn, "q_proj", False)
    kp, kh, ks = matmul_heads(xpb, xsb, attn_w_qkv, 0, d_attn, d_attn, dh, 512, d_attn, "k_proj", True)
    vp, vh, vs = matmul_heads(xpb, xsb, attn_w_qkv, 0, 2 * d_attn, d_attn, dh, 1024, 1024, "v_proj", True)
    op = sb_attention_prompt(qh, kh, vh, bias, bp, seq, 256)
    ks4 = ks.reshape(bs, ts_, N_HEADS, dh)
    vs4 = vs.reshape(bs, ts_, N_HEADS, dh)
    o_pad = sb_attention_sample(qs.reshape(bs, ts_, d_attn), ks4, vs4, bias, cache_k[0], cache_v[0],
                                page_table, 8)
    os_ = o_pad[:, :ts_, :].reshape(ms, d_attn)
    xp, xpb, xs, xsb = matmul_residual_ln(op, os_, attn_w_o, 0, xp, xs, ln_g, ln_b, 2, 512, "attn_out_ln")
    xp, _, xs, _ = ffn(xp, xpb, xs, xsb, 1)

    return (xp.reshape(bp, seq, d), xs.reshape(bs, ts_, d),
            conv_p[None], conv_s[None], pool_p[None], pool_s[None],
            kp.reshape(1, bp, seq, N_HEADS, dh), vp.reshape(1, bp, seq, N_HEADS, dh),
            ks4[None], vs4[None])
```

```python
import functools

import jax
import jax.numpy as jnp
from jax import lax
from jax.experimental import pallas as pl
from jax.experimental.pallas import tpu as pltpu

F32 = jnp.float32
BF16 = jnp.bfloat16

CONV_WIDTH = 3
POOL_WINDOWS = (2, 4, 8, 16)
POOL_BUF = max(POOL_WINDOWS) - 1
N_HEADS = 16
PAGE_SIZE = 128
DEPTH = 2
ALPHA = (2.0 * DEPTH) ** 0.25
LN_EPS = 1e-5
MASKED_LOGIT = -1e30
STEPS_PER_TRIP = 40

VMEM_LIMIT_BYTES = 60 * 1024 * 1024
SUBLANES = 8


def _params(n_axes):
    return pltpu.CompilerParams(
        dimension_semantics=("arbitrary",) * n_axes,
        vmem_limit_bytes=VMEM_LIMIT_BYTES,
    )


def _dot(x, w):
    return jnp.dot(x.astype(BF16), w, preferred_element_type=F32)


def _mm_kernel(xm_ref, xe_ref, w_ref, om_ref, oe_ref, wb_ref):
    @pl.when(pl.program_id(1) == 0)
    def _():
        wb_ref[...] = w_ref[...].astype(BF16)
        oe_ref[...] = _dot(xe_ref[...], wb_ref[...]).astype(oe_ref.dtype)

    om_ref[...] = _dot(xm_ref[...], wb_ref[...]).astype(om_ref.dtype)


def matmul(xm, xe, w, layer, col0, n, out_dtype, tm, tn, name):
    m, k = xm.shape
    me = xe.shape[0]
    cb0 = col0 // tn
    return pl.pallas_call(
        _mm_kernel,
        grid=(n // tn, m // tm),
        in_specs=[pl.BlockSpec((tm, k), lambda j, i: (i, 0)),
                  pl.BlockSpec((me, k), lambda j, i: (0, 0)),
                  pl.BlockSpec((None, k, tn), lambda j, i: (layer, 0, cb0 + j))],
        out_specs=[pl.BlockSpec((tm, tn), lambda j, i: (i, j)),
                   pl.BlockSpec((me, tn), lambda j, i: (0, j))],
        out_shape=[jax.ShapeDtypeStruct((m, n), out_dtype), jax.ShapeDtypeStruct((me, n), out_dtype)],
        scratch_shapes=[pltpu.VMEM((k, tn), BF16)],
        compiler_params=_params(2),
        name=name,
    )(xm, xe, w)


def _mm_heads_kernel(xm_ref, xe_ref, w_ref, *refs, dh, row_major):
    om_ref = refs[0] if row_major else None
    oh_ref, oe_ref, wb_ref = refs[-3:]

    @pl.when(pl.program_id(1) == 0)
    def _():
        wb_ref[...] = w_ref[...].astype(BF16)
        oe_ref[...] = _dot(xe_ref[...], wb_ref[...])

    res = _dot(xm_ref[...], wb_ref[...])
    if row_major:
        om_ref[...] = res
    for h in range(oh_ref.shape[0]):
        oh_ref[h] = res[:, h * dh:(h + 1) * dh].astype(BF16)


def matmul_heads(xm, xe, w, layer, col0, n, dh, tm, tn, name, row_major):
    m, k = xm.shape
    me = xe.shape[0]
    cb0 = col0 // tn
    hpt = tn // dh
    out_specs = [pl.BlockSpec((hpt, tm, dh), lambda j, i: (j, i, 0)),
                 pl.BlockSpec((me, tn), lambda j, i: (0, j))]
    out_shape = [jax.ShapeDtypeStruct((n // dh, m, dh), BF16), jax.ShapeDtypeStruct((me, n), F32)]
    if row_major:
        out_specs.insert(0, pl.BlockSpec((tm, tn), lambda j, i: (i, j)))
        out_shape.insert(0, jax.ShapeDtypeStruct((m, n), F32))
    w_mode = {"pipeline_mode": pl.Buffered(1)} if n == tn else {}
    return pl.pallas_call(
        functools.partial(_mm_heads_kernel, dh=dh, row_major=row_major),
        grid=(n // tn, m // tm),
        in_specs=[pl.BlockSpec((tm, k), lambda j, i: (i, 0)),
                  pl.BlockSpec((me, k), lambda j, i: (0, 0)),
                  pl.BlockSpec((None, k, tn), lambda j, i: (layer, 0, cb0 + j), **w_mode)],
        out_specs=out_specs,
        out_shape=out_shape,
        scratch_shapes=[pltpu.VMEM((k, tn), BF16)],
        compiler_params=_params(2),
        name=name,
    )(xm, xe, w)


def _layer_norm_rows(s, g, b):
    mu = jnp.mean(s, axis=-1, keepdims=True)
    d = s - mu
    var = jnp.mean(d * d, axis=-1, keepdims=True)
    return d * lax.rsqrt(var + LN_EPS) * g + b


def _mm_ln_kernel(xm_ref, xe_ref, w_ref, rm_ref, re_ref, g_ref, b_ref, *refs, bf16_copy):
    if bf16_copy:
        ym_ref, ymb_ref, ye_ref, yeb_ref = refs[:4]
    else:
        ym_ref, ye_ref = refs[:2]
        ymb_ref = yeb_ref = None
    wb_refs = refs[4 if bf16_copy else 2:]
    wb_ref = wb_refs[0] if wb_refs else w_ref

    def rows(x_ref, r_ref, y_ref, yb_ref):
        if len(x_ref.shape) == 3:
            x = jnp.concatenate([x_ref[h] for h in range(x_ref.shape[0])], axis=1)
        else:
            x = x_ref[...]
        y = _layer_norm_rows(ALPHA * r_ref[...] + _dot(x, wb_ref[...]), g_ref[...], b_ref[...])
        y_ref[...] = y
        if yb_ref is not None:
            yb_ref[...] = y.astype(BF16)

    @pl.when(pl.program_id(0) == 0)
    def _():
        if wb_refs:
            wb_ref[...] = w_ref[...].astype(BF16)
        rows(xe_ref, re_ref, ye_ref, yeb_ref)

    rows(xm_ref, rm_ref, ym_ref, ymb_ref)


def matmul_residual_ln(xm, xe, w, layer, rm, re, ln_g, ln_b, ln_row, tm, name, bf16_copy=True):
    me, k = xe.shape
    m = xm.shape[-2]
    d = w.shape[-1]
    row = lambda i: (i, 0)
    const = lambda i: (0, 0)
    if xm.ndim == 3:
        xm_spec = pl.BlockSpec((xm.shape[0], tm, xm.shape[2]), lambda i: (0, i, 0))
    else:
        xm_spec = pl.BlockSpec((tm, k), row)
    if w.ndim == 2:
        w_spec = pl.BlockSpec((k, d), const, pipeline_mode=pl.Buffered(1))
        scratch = []
    else:
        w_spec = pl.BlockSpec((None, k, d), lambda i: (layer, 0, 0), pipeline_mode=pl.Buffered(1))
        scratch = [pltpu.VMEM((k, d), BF16)]
    ln_spec = pl.BlockSpec((None, 1, d), lambda i: (ln_row, 0, 0))
    dtypes = (F32, BF16) if bf16_copy else (F32,)
    outs = pl.pallas_call(
        functools.partial(_mm_ln_kernel, bf16_copy=bf16_copy),
        grid=(m // tm,),
        in_specs=[xm_spec, pl.BlockSpec((me, k), const), w_spec,
                  pl.BlockSpec((tm, d), row), pl.BlockSpec((me, d), const), ln_spec, ln_spec],
        out_specs=[pl.BlockSpec((tm, d), row)] * len(dtypes) + [pl.BlockSpec((me, d), const)] * len(dtypes),
        out_shape=[jax.ShapeDtypeStruct((m, d), dt) for dt in dtypes]
                  + [jax.ShapeDtypeStruct((me, d), dt) for dt in dtypes],
        scratch_shapes=scratch,
        compiler_params=_params(1),
        name=name,
    )(xm, xe, w, rm, re, ln_g, ln_b)
    return outs if bf16_copy else (outs[0], None, outs[1], None)


def _ffn_up_kernel(xm_ref, xe_ref, wg_ref, wu_ref, wd_ref, hm_ref, he_ref, wdb_ref, wgb_ref, wub_ref):
    def rows(x_ref, h_ref):
        x = x_ref[...]
        h_ref[...] = (jax.nn.silu(_dot(x, wgb_ref[...])) * _dot(x, wub_ref[...])).astype(h_ref.dtype)

    @pl.when(pl.program_id(1) == 0)
    def _():
        wgb_ref[...] = wg_ref[...].astype(BF16)
        wub_ref[...] = wu_ref[...].astype(BF16)
        rows(xe_ref, he_ref)

    wdb_ref[...] = wd_ref[...].astype(BF16)
    rows(xm_ref, hm_ref)


def ffn_up(xm, xe, wg, wu, wd, layer, tm, tf, name):
    m, k = xm.shape
    me = xe.shape[0]
    f = wg.shape[2]
    d = wd.shape[2]
    n_row_steps = m // tm
    slab = tf // n_row_steps
    col = lambda j, i: (layer, 0, j)
    return pl.pallas_call(
        _ffn_up_kernel,
        grid=(f // tf, m // tm),
        in_specs=[pl.BlockSpec((tm, k), lambda j, i: (i, 0)),
                  pl.BlockSpec((me, k), lambda j, i: (0, 0)),
                  pl.BlockSpec((None, k, tf), col), pl.BlockSpec((None, k, tf), col),
                  pl.BlockSpec((None, slab, d), lambda j, i: (layer, j * n_row_steps + i, 0))],
        out_specs=[pl.BlockSpec((tm, tf), lambda j, i: (i, j)),
                   pl.BlockSpec((me, tf), lambda j, i: (0, j)),
                   pl.BlockSpec((slab, d), lambda j, i: (j * n_row_steps + i, 0))],
        out_shape=[jax.ShapeDtypeStruct((m, f), BF16), jax.ShapeDtypeStruct((me, f), BF16),
                   jax.ShapeDtypeStruct((f, d), BF16)],
        scratch_shapes=[pltpu.VMEM((k, tf), BF16), pltpu.VMEM((k, tf), BF16)],
        compiler_params=_params(2),
        name=name,
    )(xm, xe, wg, wu, wd)


def _mixer_prompt_kernel(h_ref, wc_ref, wp_ref, sp_ref, z_ref, co_ref, po_ref, cbuf, pbuf, *, ts, dc):
    i = pl.program_id(1)
    ch, ph = SUBLANES, 2 * SUBLANES
    dp = pbuf.shape[1]
    pg = dp // len(POOL_WINDOWS)

    @pl.when(i == 0)
    def _():
        cbuf[0:ch, :] = jnp.zeros((ch, dc), F32)
        pbuf[0:ph, :] = jnp.zeros((ph, dp), F32)

    gb = h_ref[:, 0:dc]
    u = h_ref[:, dc:2 * dc] * h_ref[:, 2 * dc:3 * dc]
    pv = h_ref[:, 3 * dc:3 * dc + dp]
    cbuf[ch:ch + ts, :] = u
    pbuf[ph:ph + ts, :] = pv

    y = cbuf[ch - 2:ch - 2 + ts, :] * wc_ref[0:1, :]
    y = y + cbuf[ch - 1:ch - 1 + ts, :] * wc_ref[1:2, :]
    y = y + u * wc_ref[2:3, :]
    z_ref[:, 0:dc] = (gb * y).astype(z_ref.dtype)

    pos = i * ts + lax.broadcasted_iota(jnp.int32, (ts, 1), 0)
    for gi, w in enumerate(POOL_WINDOWS):
        lo = gi * pg
        cur = pv[:, lo:lo + pg]
        s = cur
        for back in range(1, w):
            s = s + pbuf[ph - back:ph - back + ts, lo:lo + pg]
        cnt = jnp.minimum(w, pos + 1).astype(F32)
        dlt = s / cnt - cur
        yg = jnp.dot(dlt.astype(BF16), wp_ref[gi], preferred_element_type=F32)
        z_ref[:, dc + lo:dc + lo + pg] = (yg * sp_ref[:, lo:lo + pg]).astype(z_ref.dtype)

    co_ref[...] = cbuf[ch + ts - (CONV_WIDTH - 1):ch + ts, :]
    po_ref[...] = pbuf[ph + ts - POOL_BUF:ph + ts, :]
    cbuf[0:ch, :] = cbuf[ts:ts + ch, :]
    pbuf[0:ph, :] = pbuf[ts:ts + ph, :]


def mixer_prompt(h, wc, wp, sp, batch, seq, ts):
    rows, width = h.shape
    dp = sp.shape[1]
    dc = (width - dp) // 3
    nt = seq // ts
    return pl.pallas_call(
        functools.partial(_mixer_prompt_kernel, ts=ts, dc=dc),
        grid=(batch, nt),
        in_specs=[pl.BlockSpec((ts, width), lambda b, i: (b * nt + i, 0)),
                  pl.BlockSpec(wc.shape, lambda b, i: (0, 0)),
                  pl.BlockSpec(wp.shape, lambda b, i: (0, 0, 0)),
                  pl.BlockSpec(sp.shape, lambda b, i: (0, 0))],
        out_specs=[pl.BlockSpec((ts, dc + dp), lambda b, i: (b * nt + i, 0)),
                   pl.BlockSpec((None, CONV_WIDTH - 1, dc), lambda b, i: (b, 0, 0)),
                   pl.BlockSpec((None, POOL_BUF, dp), lambda b, i: (b, 0, 0))],
        out_shape=[jax.ShapeDtypeStruct((rows, dc + dp), BF16),
                   jax.ShapeDtypeStruct((batch, CONV_WIDTH - 1, dc), F32),
                   jax.ShapeDtypeStruct((batch, POOL_BUF, dp), F32)],
        scratch_shapes=[pltpu.VMEM((ts + SUBLANES, dc), F32), pltpu.VMEM((ts + 2 * SUBLANES, dp), F32)],
        compiler_params=_params(2),
        name="mixer_prompt",
    )(h, wc, wp, sp)


def _mixer_sample_kernel(h_ref, sc_ref, spool_ref, wc_ref, wp_ref, sp_ref, z_ref, co_ref, po_ref,
                         *, nb, nt, dc, pos0):
    dp = sp_ref.shape[1]
    pg = dp // len(POOL_WINDOWS)
    nc = CONV_WIDTH - 1

    conv_ext = [sc_ref[j] for j in range(nc)]
    pool_ext = [spool_ref[j] for j in range(POOL_BUF)]
    gbs = []
    for t in range(nt):
        gbs.append(h_ref[t, :, 0:dc])
        conv_ext.append(h_ref[t, :, dc:2 * dc] * h_ref[t, :, 2 * dc:3 * dc])
        pool_ext.append(h_ref[t, :, 3 * dc:3 * dc + dp])

    for t in range(nt):
        y = conv_ext[t] * wc_ref[0:1, :]
        for kk in range(1, CONV_WIDTH):
            y = y + conv_ext[t + kk] * wc_ref[kk:kk + 1, :]
        z_ref[t, :, 0:dc] = gbs[t] * y
    for j in range(nc):
        co_ref[j] = conv_ext[nt + j]

    for gi, w in enumerate(POOL_WINDOWS):
        lo = gi * pg
        dls = []
        for t in range(nt):
            e = POOL_BUF + t
            s = pool_ext[e][:, lo:lo + pg]
            for back in range(1, w):
                s = s + pool_ext[e - back][:, lo:lo + pg]
            cnt = float(min(w, pos0 + t + 1))
            dls.append(s / cnt - pool_ext[e][:, lo:lo + pg])
        dl = jnp.concatenate(dls, axis=0).astype(BF16)
        yg = jnp.dot(dl, wp_ref[gi], preferred_element_type=F32) * sp_ref[:, lo:lo + pg]
        for t in range(nt):
            z_ref[t, :, dc + lo:dc + lo + pg] = yg[t * nb:(t + 1) * nb, :]
    for j in range(POOL_BUF):
        po_ref[j] = pool_ext[nt + j]


def mixer_sample(h, state_conv, state_pool, wc, wp, sp, nb, nt, pos0):
    rows, width = h.shape
    dp = sp.shape[1]
    dc = (width - dp) // 3
    nc = CONV_WIDTH - 1
    step_major = lambda a: jnp.swapaxes(a, 0, 1)
    z, conv_new, pool_new = pl.pallas_call(
        functools.partial(_mixer_sample_kernel, nb=nb, nt=nt, dc=dc, pos0=pos0),
        out_shape=[jax.ShapeDtypeStruct((nt, nb, dc + dp), F32),
                   jax.ShapeDtypeStruct((nc, nb, dc), F32),
                   jax.ShapeDtypeStruct((POOL_BUF, nb, dp), F32)],
        compiler_params=pltpu.CompilerParams(vmem_limit_bytes=VMEM_LIMIT_BYTES),
        name="mixer_sample",
    )(step_major(h.reshape(nb, nt, width)), step_major(state_conv), step_major(state_pool), wc, wp, sp)
    return step_major(z).reshape(rows, dc + dp), step_major(conv_new), step_major(pool_new)


def _neg_softplus(z):
    return -(jnp.maximum(z, 0.0) + jnp.log1p(jnp.exp(-jnp.abs(z))))


def _suffix_sum(ls, tri):
    hi = ls.astype(BF16)
    lo = (ls - hi.astype(F32)).astype(BF16)
    return (jnp.dot(hi, tri, preferred_element_type=F32)
            + jnp.dot(lo, tri, preferred_element_type=F32))


def _lower_tri(n):
    r = lax.broadcasted_iota(jnp.int32, (n, n), 0)
    c = lax.broadcasted_iota(jnp.int32, (n, n), 1)
    return jnp.where(r >= c, 1.0, 0.0).astype(BF16)


def _sb_prompt_kernel(ij_ref, bias_ref, qb_ref, kb_ref, vb_ref, o_ref,
                      zr_ref, z_ref, hi_ref, lo_ref, e_ref, c_ref, acc_ref, offset_ref, *, tb, scale):
    npairs = ij_ref.shape[1]
    bias = bias_ref[pl.program_id(1)]
    for ref in (zr_ref, z_ref, hi_ref, lo_ref, e_ref, c_ref, acc_ref):
        ref[...] = jnp.zeros(ref.shape, ref.dtype)
    tri = _lower_tri(tb)
    hidden = jnp.where(lax.broadcasted_iota(jnp.int32, (tb, tb), 1)
                       < lax.broadcasted_iota(jnp.int32, (tb, tb), 0), 0.0, MASKED_LOGIT)
    offset_ref[0] = jnp.full((tb, tb), bias, F32)
    offset_ref[1] = bias + hidden

    def pair(p):
        p = jnp.clip(p, 0, npairs - 1)
        return ij_ref[0, p], ij_ref[1, p]

    def block_rows(ref, blk):
        return ref[pl.ds(pl.multiple_of(blk * tb, tb), tb), :]

    def step(n, s):
        t = 1 - s
        i, j = pair(n)
        zr_ref[s] = lax.dot_general(block_rows(qb_ref, i), block_rows(kb_ref, j),
                                    (((1,), (1,)), ((), ())), preferred_element_type=F32)

        i, j = pair(n - 1)
        z = zr_ref[t] * scale + offset_ref[(i == j).astype(jnp.int32)]
        nl = jnp.maximum(z, 0.0) + jnp.log(1.0 + jnp.exp(-jnp.abs(z)))
        hi = nl.astype(BF16)
        z_ref[t] = z
        hi_ref[t] = hi
        lo_ref[t] = (nl - hi.astype(F32)).astype(BF16)

        i, j = pair(n - 2)
        rc = (jnp.dot(hi_ref[s], tri, preferred_element_type=F32)
              + jnp.dot(lo_ref[s], tri, preferred_element_type=F32))
        c = jnp.where(i == j, 0.0, c_ref[...])
        e_ref[s] = jnp.exp(z_ref[s] - rc - c).astype(BF16)
        c_ref[...] = c + rc[:, 0:1]

        i, j = pair(n - 3)
        acc = jnp.where(i == j, 0.0, acc_ref[...])
        new = acc + jnp.dot(e_ref[t], block_rows(vb_ref, j), preferred_element_type=F32)
        acc = jnp.where(n - 3 < npairs, new, acc)
        acc_ref[...] = acc
        o_ref[pl.ds(pl.multiple_of(i * tb, tb), tb), :] = acc.astype(o_ref.dtype)

    n_steps = npairs + 3
    per_trip = min(n_steps, STEPS_PER_TRIP)

    def trip(m, carry):
        for r in range(per_trip):
            step(per_trip * m + r, r % 2)
        return carry

    lax.fori_loop(0, -(-n_steps // per_trip), trip, 0)


def sb_attention_prompt(q, k, v, bias, batch, seq, tb):
    nh, rows, dh = q.shape
    nblk = seq // tb
    pairs = [(i, j) for i in range(nblk) for j in range(i, -1, -1)]
    ij = jnp.asarray(pairs, dtype=jnp.int32).T
    blk = pl.BlockSpec((None, seq, dh), lambda b, h: (h, b, 0))
    smem = pl.BlockSpec(memory_space=pltpu.SMEM)
    stage_f32 = pltpu.VMEM((2, tb, tb), F32)
    stage_bf16 = pltpu.VMEM((2, tb, tb), BF16)
    return pl.pallas_call(
        functools.partial(_sb_prompt_kernel, tb=tb, scale=dh ** -0.5),
        grid=(batch, nh),
        in_specs=[smem, smem, blk, blk, blk],
        out_specs=blk,
        out_shape=jax.ShapeDtypeStruct((nh, rows, dh), BF16),
        scratch_shapes=[stage_f32, stage_f32, stage_bf16, stage_bf16, stage_bf16,
                        pltpu.VMEM((tb, 1), F32), pltpu.VMEM((tb, dh), F32), stage_f32],
        compiler_params=_params(2),
        name="sb_attention_prompt",
    )(ij, bias, q, k, v)


def _sb_sample_kernel(pt_ref, q_ref, kn_ref, vn_ref, bias_ref, *refs, n_pg, scale):
    del pt_ref
    nt, nh, dh = kn_ref.shape
    ngrp = nh // SUBLANES
    k_refs = refs[:n_pg * ngrp]
    v_refs = refs[n_pg * ngrp:2 * n_pg * ngrp]
    o_ref, qf_ref, c_ref, acc_ref = refs[2 * n_pg * ngrp:]

    def head_rows(page_refs, p, h):
        flat = page_refs[p * ngrp + h // SUBLANES].reshape(PAGE_SIZE * SUBLANES, dh)
        return flat[pl.ds(h % SUBLANES, PAGE_SIZE, stride=SUBLANES), :]

    rpad = qf_ref.shape[0] // nh
    g = pl.program_id(1)
    bias_col = bias_ref[...]

    @pl.when(g == 0)
    def _():
        for h in range(nh):
            qf_ref[h * rpad:(h + 1) * rpad, :] = q_ref[:, h * dh:(h + 1) * dh]
        rr = lax.broadcasted_iota(jnp.int32, (nh * rpad, nh), 0)
        hh = lax.broadcasted_iota(jnp.int32, (nh * rpad, nh), 1)
        rep = jnp.where(rr // rpad == hh, 1.0, 0.0).astype(BF16)
        step = lax.broadcasted_iota(jnp.int32, (nh * rpad, 1), 0) % rpad
        qr = qf_ref[...].astype(BF16).astype(F32)
        zs, lss, valids = [], [], []
        for s in range(nt):
            kr = jnp.dot(rep, kn_ref[s].astype(BF16), preferred_element_type=F32)
            z = jnp.sum(qr * kr, axis=1, keepdims=True) * scale + bias_col
            valid = step > s
            zs.append(z)
            valids.append(valid)
            lss.append(jnp.where(valid, _neg_softplus(z), 0.0))
        acc = jnp.zeros(acc_ref.shape, F32)
        rc = jnp.zeros((nh * rpad, 1), F32)
        for s in reversed(range(nt)):
            rc = rc + lss[s]
            a = jnp.where(valids[s], jnp.exp(zs[s] + rc), 0.0)
            vr = jnp.dot(rep, vn_ref[s].astype(BF16), preferred_element_type=F32)
            acc = acc + a.astype(BF16).astype(F32) * vr
        acc_ref[...] = acc
        c_ref[...] = rc

    bf16_values = lambda x: x.astype(BF16).astype(F32)
    width = n_pg * PAGE_SIZE
    zrows = []
    for h in range(nh):
        qh = bf16_values(qf_ref[h * rpad:(h + 1) * rpad, :])
        kh = jnp.concatenate([head_rows(k_refs, p, h) for p in range(n_pg)], axis=0)
        zrows.append(lax.dot_general(qh, kh, (((1,), (1,)), ((), ())), preferred_element_type=F32))
    z = jnp.concatenate(zrows, axis=0) * scale + bias_col
    ls = _neg_softplus(z)
    tb = 2 * PAGE_SIZE
    tri = _lower_tri(tb)
    c = c_ref[...]
    a_blocks = [None] * (width // tb)
    for blk in reversed(range(width // tb)):
        rc = _suffix_sum(ls[:, blk * tb:(blk + 1) * tb], tri)
        a_blocks[blk] = bf16_values(jnp.exp(z[:, blk * tb:(blk + 1) * tb] + rc + c))
        c = c + rc[:, 0:1]
    c_ref[...] = c
    a = jnp.concatenate(a_blocks, axis=1)
    orows = []
    for h in range(nh):
        vh = jnp.concatenate([head_rows(v_refs, p, h) for p in range(n_pg)], axis=0)
        orows.append(jnp.dot(a[h * rpad:(h + 1) * rpad, :], vh, preferred_element_type=F32))
    acc_ref[...] += jnp.concatenate(orows, axis=0)

    @pl.when(g == pl.num_programs(1) - 1)
    def _():
        for h in range(nh):
            o_ref[:, h * dh:(h + 1) * dh] = acc_ref[h * rpad:(h + 1) * rpad, :]


def sb_attention_sample(q, k_new, v_new, bias, cache_k, cache_v, page_table, n_pg):
    nb, nt, d_attn = q.shape
    nh, dh = k_new.shape[2], k_new.shape[3]
    n_pages = page_table.shape[1]
    rpad = SUBLANES
    q_pad = jnp.pad(q, ((0, 0), (0, rpad - nt), (0, 0)))
    bias_col = jnp.repeat(bias.astype(F32), rpad).reshape(nh * rpad, 1)
    n_steps = n_pages // n_pg

    ngrp = nh // SUBLANES
    cache_k = cache_k.reshape(-1, PAGE_SIZE, ngrp, SUBLANES, dh)
    cache_v = cache_v.reshape(-1, PAGE_SIZE, ngrp, SUBLANES, dh)

    def page_spec(p, grp):
        return pl.BlockSpec(
            (None, PAGE_SIZE, None, SUBLANES, dh),
            lambda b, g, pt: (pt[b, n_pages - (g + 1) * n_pg + p], 0, grp, 0, 0))

    page_specs = [page_spec(p, grp) for p in range(n_pg) for grp in range(ngrp)]

    per_seq3 = lambda b, g, pt: (b, 0, 0)
    per_seq4 = lambda b, g, pt: (b, 0, 0, 0)
    grid_spec = pltpu.PrefetchScalarGridSpec(
        num_scalar_prefetch=1,
        grid=(nb, n_steps),
        in_specs=[pl.BlockSpec((None, rpad, d_attn), per_seq3),
                  pl.BlockSpec((None, nt, nh, dh), per_seq4),
                  pl.BlockSpec((None, nt, nh, dh), per_seq4),
                  pl.BlockSpec((nh * rpad, 1), lambda b, g, pt: (0, 0))]
                 + page_specs * 2,
        out_specs=pl.BlockSpec((None, rpad, d_attn), per_seq3),
        scratch_shapes=[pltpu.VMEM((nh * rpad, dh), F32), pltpu.VMEM((nh * rpad, 1), F32),
                        pltpu.VMEM((nh * rpad, dh), F32)],
    )
    return pl.pallas_call(
        functools.partial(_sb_sample_kernel, n_pg=n_pg, scale=dh ** -0.5),
        grid_spec=grid_spec,
        out_shape=jax.ShapeDtypeStruct((nb, rpad, d_attn), F32),
        compiler_params=_params(2),
        name="sb_attention_sample",
    )(page_table, q_pad, k_new, v_new, bias_col,
      *([cache_k] * len(page_specs)), *([cache_v] * len(page_specs)))


def kernel(x_prompt, x_sample, state_conv, state_pool, cache_k, cache_v, page_table, ln_g, ln_b, mix_w_in, conv_w, pool_w, pool_scale, mix_w_out, attn_w_qkv, attn_w_o, attn_bias, ffn_w_gate, ffn_w_up, ffn_w_down):
    bp, seq, d = x_prompt.shape
    bs, ts_, _ = x_sample.shape
    mp, ms = bp * seq, bs * ts_
    past_len = page_table.shape[1] * PAGE_SIZE
    xp = x_prompt.reshape(mp, d)
    xs = x_sample.reshape(ms, d)
    dp = pool_scale.shape[1]
    dc = conv_w.shape[2]
    d_attn = attn_w_o.shape[1]
    dh = d_attn // N_HEADS
    ln_g = ln_g.reshape(-1, 1, d)
    ln_b = ln_b.reshape(-1, 1, d)

    def ffn(xp, xpb, xs, xsb, layer):
        hp, hs, wd = ffn_up(xpb, xsb, ffn_w_gate, ffn_w_up, ffn_w_down, layer, 1024, 512, f"ffn_up_l{layer}")
        last = layer == DEPTH - 1
        tm = 512 if last else 256
        return matmul_residual_ln(hp, hs, wd, 0, xp, xs, ln_g, ln_b, 2 * layer + 1, tm,
                                  f"ffn_down_ln_l{layer}", bf16_copy=not last)

    wp = pool_w[0].astype(BF16)
    sp = pool_scale[0].reshape(1, dp)
    hp, hs = matmul(xp, xs, mix_w_in, 0, 0, mix_w_in.shape[2], F32, 1024, 1024, "mix_in")
    zp, conv_p, pool_p = mixer_prompt(hp, conv_w[0], wp, sp, bp, seq, 1024)
    zs, conv_s, pool_s = mixer_sample(hs, state_conv[0], state_pool[0], conv_w[0], wp, sp, bs, ts_, past_len)
    xp, xpb, xs, xsb = matmul_residual_ln(zp, zs, mix_w_out, 0, xp, xs, ln_g, ln_b, 0, 512, "mix_out_ln")
    xp, xpb, xs, xsb = ffn(xp, xpb, xs, xsb, 0)

    bias = attn_bias[0]
    qh, qs = matmul_heads(xpb, xsb, attn_w_qkv, 0, 0, d_attn, dh, 1024, d_attn, "q_proj", False)
    kp, kh, ks = matmul_heads(xpb, xsb, attn_w_qkv, 0, d_attn, d_attn, dh, 512, d_attn, "k_proj", True)
    vp, vh, vs = matmul_heads(xpb, xsb, attn_w_qkv, 0, 2 * d_attn, d_attn, dh, 1024, 1024, "v_proj", True)
    op = sb_attention_prompt(qh, kh, vh, bias, bp, seq, 256)
    ks4 = ks.reshape(bs, ts_, N_HEADS, dh)
    vs4 = vs.reshape(bs, ts_, N_HEADS, dh)
    o_pad = sb_attention_sample(qs.reshape(bs, ts_, d_attn), ks4, vs4, bias, cache_k[0], cache_v[0],
                                page_table, 8)
    os_ = o_pad[:, :ts_, :].reshape(ms, d_attn)
    xp, xpb, xs, xsb = matmul_residual_ln(op, os_, attn_w_o, 0, xp, xs, ln_g, ln_b, 2, 512, "attn_out_ln")
    xp, _, xs, _ = ffn(xp, xpb, xs, xsb, 1)

    return (xp.reshape(bp, seq, d), xs.reshape(bs, ts_, d),
            conv_p[None], conv_s[None], pool_p[None], pool_s[None],
            kp.reshape(1, bp, seq, N_HEADS, dh), vp.reshape(1, bp, seq, N_HEADS, dh),
            ks4[None], vs4[None])
```
